```python
import jax, jax.numpy as jnp
from jax import lax
import numpy as np

D_MODEL = 1024
BATCH = 16
SEQ = 4096
DEPTH = 1

A_HEADS = D_MODEL // 128
A_HEAD_DIM = 64
A_WIDTH = A_HEADS * A_HEAD_DIM
A_PATTERNS = ((128, 1), (512, 4), (2048, 16))
BLK = 128
NEG = -1e30
B_HEADS = D_MODEL // 256
B_EXPAND = 128
B_HEAD_V = 128
B_WIDTH_K = B_HEADS * B_EXPAND
B_WIDTH = B_HEADS * B_HEAD_V
B_CHUNK = 64
FFN_HIDDEN = ((8 * D_MODEL // 3 + 255) // 256) * 256
PLE_DIM = 256
EPS = 1e-6
IN_SIZES = (A_WIDTH, A_WIDTH, A_WIDTH,
            B_WIDTH_K, B_WIDTH_K, B_WIDTH, B_WIDTH,
            D_MODEL, D_MODEL)
IN_TOTAL = sum(IN_SIZES)
IN_SPLITS = [int(s) for s in np.cumsum(IN_SIZES)[:-1]]

kernel_name = "hybrid_dilated_attn_hgrn2_gated_merge"


def rmsnorm(x, g):
    xf = x.astype(jnp.float32)
    y = xf * lax.rsqrt(jnp.mean(xf * xf, axis=-1, keepdims=True) + EPS)
    return (y * g.astype(jnp.float32)).astype(x.dtype)


def _dilated_branch(q, k, v, window, dilation):
    B, H, S, Dh = q.shape
    span = window // dilation
    unit = dilation * BLK
    s_pad = -(-S // unit) * unit
    pad = s_pad - S
    L = s_pad // dilation
    nb = L // BLK

    def to_blocks(t):
        t = jnp.pad(t, ((0, 0), (0, 0), (pad, 0), (0, 0)))
        t = t.reshape(B, H, L, dilation, Dh).transpose(0, 1, 3, 2, 4)
        return t.reshape(B, H, dilation, nb, BLK, Dh)

    qb, kb, vb = to_blocks(q), to_blocks(k), to_blocks(v)

    def band(t):
        prev = jnp.pad(t[:, :, :, :-1], ((0, 0), (0, 0), (0, 0), (1, 0), (0, 0), (0, 0)))
        return jnp.concatenate([prev, t], axis=4)

    kk, vv = band(kb), band(vb)
    s = jnp.einsum("bhrnqd,bhrnkd->bhrnqk", qb, kk) * (Dh ** -0.5)
    a = jnp.arange(BLK)[:, None]
    c = jnp.arange(2 * BLK)[None, :]
    rel = a + BLK - c
    r = jnp.arange(dilation)[:, None, None, None]
    n = jnp.arange(nb)[None, :, None, None]
    key_pos = ((n - 1) * BLK + c) * dilation + r - pad
    mask = (rel >= 0) & (rel <= span) & (key_pos >= 0)
    s = jnp.where(mask, s, NEG)
    m = jnp.max(s, axis=-1)
    e = jnp.where(mask, jnp.exp(s - m[..., None]), 0.0)
    den = jnp.sum(e, axis=-1)
    num = jnp.einsum("bhrnqk,bhrnkd->bhrnqd", e, vv)

    def from_blocks(t):
        tail = t.shape[5:]
        t = t.reshape((B, H, dilation, L) + tail)
        t = jnp.swapaxes(t, 2, 3).reshape((B, H, s_pad) + tail)
        return t[:, :, pad:]

    return from_blocks(m), from_blocks(den), from_blocks(num)


def dilated_attention(qa, ka, va):
    B, S, _ = qa.shape
    heads = lambda t: t.astype(jnp.float32).reshape(B, S, A_HEADS, A_HEAD_DIM).transpose(0, 2, 1, 3)
    q, k, v = heads(qa), heads(ka), heads(va)
    ms, dens, nums = [], [], []
    for window, dilation in A_PATTERNS:
        m, den, num = _dilated_branch(q, k, v, window, dilation)
        ms.append(m); dens.append(den); nums.append(num)
    ms = jnp.stack(ms); dens = jnp.stack(dens); nums = jnp.stack(nums)
    w = jnp.exp(ms - jnp.max(ms, axis=0, keepdims=True))
    out = jnp.sum(w[..., None] * nums, axis=0) / jnp.sum(w * dens, axis=0)[..., None]
    return out.transpose(0, 2, 1, 3).reshape(B, S, A_WIDTH).astype(qa.dtype)


def hgrn2(qb, fb, ib, gb, lb, g_norm):
    B, S, _ = qb.shape
    heads = lambda t: t.astype(jnp.float32).reshape(B, S, B_HEADS, -1)
    lb = lb.astype(jnp.float32).reshape(B_HEADS, B_EXPAND)
    q = heads(qb)
    f = lb + (1.0 - lb) * jax.nn.sigmoid(heads(fb))
    k = 1.0 - f
    logf = jnp.log(f)
    v = heads(ib)
    nc = S // B_CHUNK
    chunks = lambda t: t.reshape(B, nc, B_CHUNK, B_HEADS, -1).transpose(1, 0, 3, 2, 4)
    tril = jnp.tril(jnp.ones((B_CHUNK, B_CHUNK), dtype=bool))[:, :, None]

    def step(state, inp):
        qc, kc, vc, gc = inp
        b = jnp.cumsum(gc, axis=2)
        inter = jnp.einsum("bhtk,bhkv->bhtv", qc * jnp.exp(b), state)
        diff = b[:, :, :, None, :] - b[:, :, None, :, :]
        decay = jnp.exp(jnp.where(tril, diff, -jnp.inf))
        scores = jnp.einsum("bhtk,bhsk,bhtsk->bhts", qc, kc, decay)
        intra = jnp.einsum("bhts,bhsv->bhtv", scores, vc)
        b_end = b[:, :, -1, :]
        new_state = jnp.exp(b_end)[..., None] * state + jnp.einsum(
            "bhsk,bhsv->bhkv", kc * jnp.exp(b_end[:, :, None, :] - b), vc)
        return new_state, inter + intra

    state0 = jnp.zeros((B, B_HEADS, B_EXPAND, B_HEAD_V), jnp.float32)
    _, o = lax.scan(step, state0, (chunks(q), chunks(k), chunks(v), chunks(logf)))
    o = o.transpose(1, 0, 3, 2, 4).reshape(B, S, B_HEADS, B_HEAD_V)
    o = rmsnorm(o, g_norm) * jax.nn.silu(heads(gb))
    return o.reshape(B, S, B_WIDTH).astype(qb.dtype)


def setup_inputs(seed: int = 0) -> dict:
    key = jax.random.key(seed)
    ks = jax.random.split(key, 20)
    nrm = lambda k, shape, fan_in: jax.random.normal(k, shape, jnp.float32) * (fan_in ** -0.5)
    gain = lambda k, shape: 1.0 + 0.01 * jax.random.normal(k, shape, jnp.float32)
    return {
        "x": jax.random.normal(ks[0], (BATCH, SEQ, D_MODEL), jnp.float32),
        "p": jax.random.normal(ks[1], (DEPTH, BATCH, SEQ, PLE_DIM), jnp.float32),
        "norm_mix": gain(ks[2], (DEPTH, D_MODEL)),
        "w_in": nrm(ks[3], (DEPTH, D_MODEL, IN_TOTAL), D_MODEL),
        "hg_lb": 0.1 * jax.random.normal(ks[4], (DEPTH + 1, B_WIDTH_K), jnp.float32),
        "hg_norm": gain(ks[5], (DEPTH, B_HEAD_V)),
        "w_a_up": nrm(ks[6], (DEPTH, A_WIDTH, D_MODEL), A_WIDTH),
        "w_b_up": nrm(ks[7], (DEPTH, B_WIDTH, D_MODEL), B_WIDTH),
        "w_out": nrm(ks[8], (DEPTH, D_MODEL, D_MODEL), D_MODEL),
        "norm_ffn": gain(ks[9], (DEPTH, D_MODEL)),
        "w_gu": nrm(ks[10], (DEPTH, D_MODEL, 2 * FFN_HIDDEN), D_MODEL),
        "w_down": nrm(ks[11], (DEPTH, FFN_HIDDEN, D_MODEL), FFN_HIDDEN),
        "norm_ple": gain(ks[12], (DEPTH, D_MODEL)),
        "w_pe": nrm(ks[13], (DEPTH, PLE_DIM, D_MODEL), PLE_DIM),
        "w_pg": nrm(ks[14], (DEPTH, D_MODEL, D_MODEL), D_MODEL),
        "norm_final": gain(ks[15], (D_MODEL,)),
    }


def reference(x, p, norm_mix, w_in, hg_lb, hg_norm, w_a_up, w_b_up, w_out, norm_ffn,
              w_gu, w_down, norm_ple, w_pe, w_pg, norm_final):
    lb_all = jnp.cumsum(jax.nn.softmax(hg_lb.astype(jnp.float32), axis=0), axis=0)
    for i in range(DEPTH):
        h = rmsnorm(x, norm_mix[i])
        proj = h @ w_in[i]
        qa, ka, va, qb, fb, ib, gb, gate_a, gate_b = jnp.split(proj, IN_SPLITS, axis=-1)
        ya = dilated_attention(qa, ka, va)
        yb = hgrn2(qb, fb, ib, gb, lb_all[i], hg_norm[i])
        merged = jax.nn.sigmoid(gate_a) * (ya @ w_a_up[i]) + jax.nn.sigmoid(gate_b) * (yb @ w_b_up[i])
        x = x + merged @ w_out[i]
        h = rmsnorm(x, norm_ffn[i])
        g, u = jnp.split(h @ w_gu[i], 2, axis=-1)
        x = x + (jax.nn.silu(g) * u) @ w_down[i]
        hp = rmsnorm(x, norm_ple[i])
        x = x + (p[i] @ w_pe[i]) * jax.nn.sigmoid(hp @ w_pg[i])
    return rmsnorm(x, norm_final)
```

```python
import functools

import jax
import jax.numpy as jnp
import numpy as np
from jax import lax
from jax.experimental import pallas as pl
from jax.experimental.pallas import tpu as pltpu

F32 = jnp.float32
BF16 = jnp.bfloat16

LANES = 128
EPS = 1e-6
NEG = -1e30

A_HEAD_DIM = 64
A_PATTERNS = ((128, 1), (512, 4), (2048, 16))
A_BLK = 128
B_HEAD_DIM = 128
HG_TILE = 256

VMEM_LIMIT = 56 * 1024 * 1024


def _params(*sem):
    return pltpu.CompilerParams(dimension_semantics=sem, vmem_limit_bytes=VMEM_LIMIT)


def _const_spec(shape):
    nd = len(shape)
    return pl.BlockSpec(shape, lambda *_: (0,) * nd, pipeline_mode=pl.Buffered(1))


def _dot(a, b):
    return jnp.dot(a, b, preferred_element_type=F32)


def _dot_nt(a, b):
    return lax.dot_general(a, b, (((1,), (1,)), ((), ())), preferred_element_type=F32)


def _dot_tn(a, b):
    return lax.dot_general(a, b, (((0,), (0,)), ((), ())), preferred_element_type=F32)


def _rms(x, g):
    return x * lax.rsqrt(jnp.mean(x * x, axis=-1, keepdims=True) + EPS) * g


def _proj_kernel(x_ref, g_ref, w_ref, qa_ref, ka_ref, va_ref, qb_ref, fb_ref, ib_ref, gb_ref,
                 ga_ref, gg_ref, *, width):
    h = _rms(x_ref[...], g_ref[...]).astype(BF16)
    col = 0

    def proj(n):
        nonlocal col
        y = _dot(h, w_ref[:, col:col + n])
        col += n
        return y

    qa_ref[...] = (proj(width) * (A_HEAD_DIM ** -0.5)).astype(BF16)
    ka_ref[...] = proj(width).astype(BF16)
    va_ref[...] = proj(width).astype(BF16)
    qb_ref[...] = proj(width).astype(BF16)
    fb_ref[...] = proj(width)
    ib_ref[...] = proj(width).astype(BF16)
    gb_ref[...] = proj(width).astype(BF16)
    ga_ref[...] = jax.nn.sigmoid(proj(2 * width)).astype(BF16)
    gg_ref[...] = jax.nn.sigmoid(proj(2 * width)).astype(BF16)


def _proj_call(x2, g, w, *, tm):
    m, d = x2.shape
    width = d // 2
    row = lambda n: pl.BlockSpec((tm, n), lambda i: (i, 0))
    sds = lambda n, dt: jax.ShapeDtypeStruct((m, n), dt)
    return pl.pallas_call(
        functools.partial(_proj_kernel, width=width),
        grid=(m // tm,),
        in_specs=[row(d), _const_spec((1, d)), _const_spec(w.shape)],
        out_specs=[row(width)] * 7 + [row(d)] * 2,
        out_shape=[sds(width, BF16)] * 4 + [sds(width, F32)] + [sds(width, BF16)] * 2
        + [sds(d, BF16)] * 2,
        compiler_params=_params("parallel"),
        name="proj",
    )(x2, g, w)


def _attn_kernel(bias_ref, q_ref, k_ref, v_ref, o_ref, lse_ref, *, tq):
    width = q_ref.shape[2]
    pairs = width // LANES
    lane = lax.broadcasted_iota(jnp.int32, (A_BLK, LANES), 1)
    low_half = lane < A_HEAD_DIM
    ones = jnp.ones((2 * A_BLK, LANES), BF16)
    for qi in range(tq // A_BLK):
        n = pl.program_id(2) * (tq // A_BLK) + qi
        first = n == 0
        start = pl.multiple_of(jnp.maximum(n - 1, 0) * A_BLK, A_BLK)
        bias = bias_ref[jnp.where(first, 1, 0)]
        rows = pl.ds(qi * A_BLK, A_BLK)
        lse_tile = jnp.zeros((A_BLK, LANES), F32)
        for p in range(pairs):
            cols = pl.ds(p * LANES, LANES)
            q_pair = q_ref[0, rows, cols]
            k_band = k_ref[0, pl.ds(start, 2 * A_BLK), cols]
            v_cat = jnp.concatenate([v_ref[0, pl.ds(start, 2 * A_BLK), cols], ones], axis=1)
            halves = []
            for hh in range(2):
                mine = low_half if hh == 0 else jnp.logical_not(low_half)
                s = _dot_nt(jnp.where(mine, q_pair, jnp.zeros_like(q_pair)), k_band) + bias
                mx = jnp.max(s, axis=-1, keepdims=True)
                e = jnp.exp(s - mx).astype(BF16)
                r = _dot(e, v_cat)
                den = r[:, LANES:]
                halves.append(r[:, :LANES] / den)
                lse_tile = jnp.where(lane == 2 * p + hh, mx + jnp.log(den), lse_tile)
            o_ref[0, rows, cols] = jnp.where(low_half, halves[0], halves[1]).astype(o_ref.dtype)
        lse_ref[0, rows, :] = lse_tile


def _band_bias(span):
    a = np.arange(A_BLK)[:, None]
    c = np.arange(2 * A_BLK)[None, :]
    rel = a + A_BLK - c
    normal = (rel >= 0) & (rel <= span)
    rel0 = a - c
    first = (rel0 >= 0) & (rel0 <= span)
    return jnp.asarray(np.where(np.stack([normal, first]), 0.0, NEG), F32)


def _attn_call(q, k, v, *, batch, seq, window, dilation):
    width = q.shape[1]
    assert seq % (dilation * A_BLK) == 0 and window % dilation == 0
    length = seq // dilation
    tq = min(length, 4 * A_BLK)
    sub = lambda t: t.reshape(batch, length, dilation * width)
    o, lse = pl.pallas_call(
        functools.partial(_attn_kernel, tq=tq),
        grid=(batch, dilation, length // tq),
        in_specs=[
            _const_spec((2, A_BLK, 2 * A_BLK)),
            pl.BlockSpec((1, tq, width), lambda b, r, t: (b, t, r)),
            pl.BlockSpec((1, length, width), lambda b, r, t: (b, 0, r)),
            pl.BlockSpec((1, length, width), lambda b, r, t: (b, 0, r)),
        ],
        out_specs=[
            pl.BlockSpec((1, tq, width), lambda b, r, t: (b, t, r)),
            pl.BlockSpec((1, tq, LANES), lambda b, r, t: (b, t, r)),
        ],
        out_shape=[
            jax.ShapeDtypeStruct((batch, length, dilation * width), BF16),
            jax.ShapeDtypeStruct((batch, length, dilation * LANES), F32),
        ],
        compiler_params=_params("parallel", "parallel", "arbitrary"),
        name=f"attn_d{dilation}",
    )(_band_bias(window // dilation), sub(q), sub(k), sub(v))
    return o.reshape(batch * seq, width), lse.reshape(batch * seq, LANES)


def _hgrn_kernel(lvl_ref, hglb_ref, gn_ref, q_ref, f_ref, i_ref, g_ref, y_ref, state_ref):
    tile, width = q_ref.shape
    heads = width // B_HEAD_DIM

    @pl.when(pl.program_id(1) == 0)
    def _():
        state_ref[...] = jnp.zeros_like(state_ref)

    hg = hglb_ref[...]
    ex = jnp.exp(hg - jnp.max(hg, axis=0, keepdims=True))
    lb_all = ex[0:1] / jnp.sum(ex, axis=0, keepdims=True)

    row = lax.broadcasted_iota(jnp.int32, (tile, B_HEAD_DIM), 0)
    lvl = lvl_ref[...]
    for h in range(heads):
        cols = pl.ds(h * B_HEAD_DIM, B_HEAD_DIM)
        lb = lb_all[:, h * B_HEAD_DIM:(h + 1) * B_HEAD_DIM]
        f = lb + (1.0 - lb) * jax.nn.sigmoid(f_ref[:, cols])
        q = q_ref[:, cols].astype(F32)
        kk = 1.0 - f
        v = i_ref[:, cols]
        pre = jnp.log(f)
        tot = pre
        scores = jnp.where(lvl == 0, _dot_nt(q.astype(BF16), kk.astype(BF16)), 0.0)
        m, level = 1, 1
        while m < tile:
            upper = (row & m) != 0
            dec = jnp.exp(jnp.where(upper, pre, tot - pre))
            ql = jnp.where(upper, q * dec, 0.0).astype(BF16)
            kl = jnp.where(upper, 0.0, kk * dec).astype(BF16)
            scores = jnp.where(lvl == level, _dot_nt(ql, kl), scores)
            below = pltpu.roll(tot, m, 0)
            above = pltpu.roll(tot, tile - m, 0)
            pre = pre + jnp.where(upper, below, 0.0)
            tot = tot + jnp.where(upper, below, above)
            m, level = 2 * m, level + 1
        st = state_ref[h]
        o = _dot_nt((q * jnp.exp(pre)).astype(BF16), st.astype(BF16)) + _dot(scores.astype(BF16), v)
        k_end = (kk * jnp.exp(tot - pre)).astype(BF16)
        state_ref[h] = st * jnp.exp(tot[0:1, :]) + _dot_tn(v, k_end)
        gate = g_ref[:, cols].astype(F32)
        y_ref[:, cols] = (_rms(o, gn_ref[...]) * (gate * jax.nn.sigmoid(gate))).astype(y_ref.dtype)


def _level_map(tile):
    t = np.arange(tile)[:, None]
    s = np.arange(tile)[None, :]
    x = np.maximum(t ^ s, 1)
    lv = np.floor(np.log2(x)).astype(np.int32) + 1
    return jnp.asarray(np.where(s < t, lv, np.where(s == t, 0, -1)), jnp.int32)


def _hgrn_call(qb, fb, ib, gb, hg_lb, g_norm, *, batch, seq):
    m, width = qb.shape
    tile = HG_TILE
    assert seq % tile == 0
    steps = seq // tile
    row = pl.BlockSpec((tile, width), lambda b, t: (b * steps + t, 0))
    return pl.pallas_call(
        _hgrn_kernel,
        grid=(batch, steps),
        in_specs=[_const_spec((tile, tile)), _const_spec(hg_lb.shape), _const_spec((1, B_HEAD_DIM)),
                  row, row, row, row],
        out_specs=row,
        out_shape=jax.ShapeDtypeStruct((m, width), BF16),
        scratch_shapes=[pltpu.VMEM((width // B_HEAD_DIM, B_HEAD_DIM, B_HEAD_DIM), F32)],
        compiler_params=_params("parallel", "arbitrary"),
        name="hgrn2",
    )(_level_map(tile), hg_lb, g_norm, qb, fb, ib, gb)


def _merge_kernel(x_ref, o1_ref, o2_ref, o3_ref, l1_ref, l2_ref, l3_ref, yb_ref, ga_ref, gg_ref,
                  ex_ref, wa_ref, wb_ref, wo_ref, out_ref):
    lses = [l1_ref[...], l2_ref[...], l3_ref[...]]
    top = jnp.maximum(jnp.maximum(lses[0], lses[1]), lses[2])
    ws = [jnp.exp(l - top) for l in lses]
    inv = 1.0 / (ws[0] + ws[1] + ws[2])
    ya = None
    for w, o_ref in zip(ws, (o1_ref, o2_ref, o3_ref)):
        term = _dot((w * inv).astype(BF16), ex_ref[...]) * o_ref[...].astype(F32)
        ya = term if ya is None else ya + term
    a = _dot(ya.astype(BF16), wa_ref[...])
    b = _dot(yb_ref[...], wb_ref[...])
    merged = ga_ref[...].astype(F32) * a + gg_ref[...].astype(F32) * b
    out_ref[...] = x_ref[...] + _dot(merged.astype(BF16), wo_ref[...])


def _merge_call(x2, os_, lses, yb, ga, gg, wa, wb, wo, *, tm):
    m, d = x2.shape
    width = yb.shape[1]
    expand = np.zeros((LANES, width), np.float32)
    for h in range(width // A_HEAD_DIM):
        expand[h, h * A_HEAD_DIM:(h + 1) * A_HEAD_DIM] = 1.0
    row = lambda n: pl.BlockSpec((tm, n), lambda i: (i, 0))
    return pl.pallas_call(
        _merge_kernel,
        grid=(m // tm,),
        in_specs=[row(d)] + [row(width)] * 3 + [row(LANES)] * 3 + [row(width), row(d), row(d)]
        + [_const_spec((LANES, width)), _const_spec(wa.shape), _const_spec(wb.shape),
           _const_spec(wo.shape)],
        out_specs=row(d),
        out_shape=jax.ShapeDtypeStruct((m, d), F32),
        compiler_params=_params("parallel"),
        name="merge",
    )(x2, *os_, *lses, yb, ga, gg, jnp.asarray(expand, BF16), wa, wb, wo)


def _ffn_kernel(x_ref, p_ref, nf_ref, np_ref, nl_ref, wg_ref, wu_ref, wd_ref, wpe_ref, wpg_ref,
                out_ref, *, chunk):
    x = x_ref[...]
    h = _rms(x, nf_ref[...]).astype(BF16)
    hidden = wg_ref.shape[1]
    acc = x
    for c in range(0, hidden, chunk):
        g = _dot(h, wg_ref[:, c:c + chunk])
        u = _dot(h, wu_ref[:, c:c + chunk])
        acc = acc + _dot((g * jax.nn.sigmoid(g) * u).astype(BF16), wd_ref[c:c + chunk, :])
    hp = _rms(acc, np_ref[...]).astype(BF16)
    pe = _dot(p_ref[...].astype(BF16), wpe_ref[...])
    x3 = acc + pe * jax.nn.sigmoid(_dot(hp, wpg_ref[...]))
    out_ref[...] = _rms(x3, nl_ref[...])


def _ffn_call(x2, p2, nf, npl, nl, wg, wu, wd, wpe, wpg, *, tm, chunk):
    m, d = x2.shape
    row = lambda n: pl.BlockSpec((tm, n), lambda i: (i, 0))
    return pl.pallas_call(
        functools.partial(_ffn_kernel, chunk=chunk),
        grid=(m // tm,),
        in_specs=[row(d), row(p2.shape[1])] + [_const_spec((1, d))] * 3
        + [_const_spec(w.shape) for w in (wg, wu, wd, wpe, wpg)],
        out_specs=row(d),
        out_shape=jax.ShapeDtypeStruct((m, d), F32),
        compiler_params=_params("parallel"),
        name="ffn",
    )(x2, p2, nf, npl, nl, wg, wu, wd, wpe, wpg)


def kernel(x, p, norm_mix, w_in, hg_lb, hg_norm, w_a_up, w_b_up, w_out, norm_ffn, w_gu, w_down,
           norm_ple, w_pe, w_pg, norm_final):
    batch, seq, d = x.shape
    depth = p.shape[0]
    assert depth == 1 and hg_lb.shape[0] == 2
    m = batch * seq
    tm = 512
    hidden = w_down.shape[1]
    bf = lambda w: w.astype(BF16)
    vec = lambda g: g.reshape(1, -1).astype(F32)

    x2 = x.reshape(m, d)
    qa, ka, va, qb, fb, ib, gb, ga, gg = _proj_call(x2, vec(norm_mix[0]), bf(w_in[0]), tm=tm)
    os_, lses = [], []
    for window, dilation in A_PATTERNS:
        o, lse = _attn_call(qa, ka, va, batch=batch, seq=seq, window=window, dilation=dilation)
        os_.append(o)
        lses.append(lse)
    yb = _hgrn_call(qb, fb, ib, gb, hg_lb.astype(F32), vec(hg_norm[0]), batch=batch, seq=seq)
    x2 = _merge_call(x2, os_, lses, yb, ga, gg, bf(w_a_up[0]), bf(w_b_up[0]), bf(w_out[0]), tm=tm)
    out = _ffn_call(x2, p[0].reshape(m, -1), vec(norm_ffn[0]), vec(norm_ple[0]), vec(norm_final),
                    bf(w_gu[0, :, :hidden]), bf(w_gu[0, :, hidden:]), bf(w_down[0]), bf(w_pe[0]),
                    bf(w_pg[0]), tm=tm, chunk=256)
    return out.reshape(batch, seq, d)
```

```python
import functools

import jax
import jax.numpy as jnp
import numpy as np
from jax import lax
from jax.experimental import pallas as pl
from jax.experimental.pallas import tpu as pltpu

F32 = jnp.float32
BF16 = jnp.bfloat16

LANES = 128
EPS = 1e-6
NEG = -1e30

A_HEAD_DIM = 64
A_PATTERNS = ((128, 1), (512, 4), (2048, 16))
A_BLK = 128
B_HEAD_DIM = 128
HG_TILE = 256

VMEM_LIMIT = 56 * 1024 * 1024


def _params(*sem):
    return pltpu.CompilerParams(dimension_semantics=sem, vmem_limit_bytes=VMEM_LIMIT)


def _const_spec(shape):
    nd = len(shape)
    return pl.BlockSpec(shape, lambda *_: (0,) * nd, pipeline_mode=pl.Buffered(1))


def _dot(a, b):
    return jnp.dot(a, b, preferred_element_type=F32)


def _dot_nt(a, b):
    return lax.dot_general(a, b, (((1,), (1,)), ((), ())), preferred_element_type=F32)


def _dot_tn(a, b):
    return lax.dot_general(a, b, (((0,), (0,)), ((), ())), preferred_element_type=F32)


def _rms(x, g):
    return x * lax.rsqrt(jnp.mean(x * x, axis=-1, keepdims=True) + EPS) * g


def _proj_kernel(x_ref, g_ref, w_ref, *refs, width, dilations):
    nd = len(dilations)
    attn_refs, rest = refs[:3 * nd], refs[3 * nd:]
    qb_ref, fb_ref, ib_ref, gb_ref, ga_ref, gg_ref, slab_ref = rest
    tm = x_ref.shape[0]
    h = _rms(x_ref[...], g_ref[...]).astype(BF16)
    col = 0

    def proj(n):
        nonlocal col
        y = _dot(h, w_ref[:, col:col + n])
        col += n
        return y

    slabs = width // LANES
    for a in range(3):
        y = proj(width)
        if a == 0:
            y = y * (A_HEAD_DIM ** -0.5)
        for c in range(slabs):
            slab_ref[c] = y[:, c * LANES:(c + 1) * LANES]
        for out_ref, d in zip(attn_refs[a * nd:(a + 1) * nd], dilations):
            if d == 1:
                out_ref[...] = y.astype(BF16)
                continue
            for r in range(d):
                for c in range(slabs):
                    out_ref[:, pl.ds((r * slabs + c) * LANES, LANES)] = (
                        slab_ref[c, pl.ds(r, tm // d, stride=d), :].astype(BF16))
    qb_ref[...] = proj(width).astype(BF16)
    fb_ref[...] = proj(width)
    ib_ref[...] = proj(width).astype(BF16)
    gb_ref[...] = proj(width).astype(BF16)
    ga_ref[...] = jax.nn.sigmoid(proj(2 * width)).astype(BF16)
    gg_ref[...] = jax.nn.sigmoid(proj(2 * width)).astype(BF16)


def _proj_call(x2, g, w, *, tm, dilations):
    m, d = x2.shape
    width = d // 2
    row = lambda n: pl.BlockSpec((tm, n), lambda i: (i, 0))
    sds = lambda n, dt: jax.ShapeDtypeStruct((m, n), dt)
    dil_specs = [pl.BlockSpec((tm // dl, dl * width), lambda i: (i, 0)) for dl in dilations] * 3
    dil_shapes = [jax.ShapeDtypeStruct((m // dl, dl * width), BF16) for dl in dilations] * 3
    return pl.pallas_call(
        functools.partial(_proj_kernel, width=width, dilations=dilations),
        grid=(m // tm,),
        in_specs=[row(d), _const_spec((1, d)), _const_spec(w.shape)],
        out_specs=dil_specs + [row(width)] * 4 + [row(d)] * 2,
        out_shape=dil_shapes + [sds(width, BF16), sds(width, F32), sds(width, BF16), sds(width, BF16)]
        + [sds(d, BF16)] * 2,
        scratch_shapes=[pltpu.VMEM((width // LANES, tm, LANES), F32)],
        compiler_params=_params("parallel"),
        name="proj",
    )(x2, g, w)


def _attn_kernel(bias_ref, q_ref, k_ref, v_ref, o_ref, lse_ref, *, tq):
    width = q_ref.shape[2]
    pairs = width // LANES
    lane = lax.broadcasted_iota(jnp.int32, (A_BLK, LANES), 1)
    low_half = lane < A_HEAD_DIM
    ones = jnp.ones((2 * A_BLK, LANES), BF16)
    for qi in range(tq // A_BLK):
        n = pl.program_id(2) * (tq // A_BLK) + qi
        first = n == 0
        start = pl.multiple_of(jnp.maximum(n - 1, 0) * A_BLK, A_BLK)
        bias = bias_ref[jnp.where(first, 1, 0)]
        rows = pl.ds(qi * A_BLK, A_BLK)
        lse_tile = jnp.zeros((A_BLK, LANES), F32)
        for p in range(pairs):
            cols = pl.ds(p * LANES, LANES)
            q_pair = q_ref[0, rows, cols]
            k_band = k_ref[0, pl.ds(start, 2 * A_BLK), cols]
            v_cat = jnp.concatenate([v_ref[0, pl.ds(start, 2 * A_BLK), cols], ones], axis=1)
            halves = []
            for hh in range(2):
                mine = low_half if hh == 0 else jnp.logical_not(low_half)
                s = _dot_nt(jnp.where(mine, q_pair, jnp.zeros_like(q_pair)), k_band) + bias
                mx = jnp.max(s, axis=-1, keepdims=True)
                e = jnp.exp(s - mx).astype(BF16)
                r = _dot(e, v_cat)
                den = r[:, LANES:]
                halves.append(r[:, :LANES] / den)
                lse_tile = jnp.where(lane == 2 * p + hh, mx + jnp.log(den), lse_tile)
            o_ref[0, rows, cols] = jnp.where(low_half, halves[0], halves[1]).astype(o_ref.dtype)
        lse_ref[0, rows, :] = lse_tile


def _band_bias(span):
    a = np.arange(A_BLK)[:, None]
    c = np.arange(2 * A_BLK)[None, :]
    rel = a + A_BLK - c
    normal = (rel >= 0) & (rel <= span)
    rel0 = a - c
    first = (rel0 >= 0) & (rel0 <= span)
    return jnp.asarray(np.where(np.stack([normal, first]), 0.0, NEG), F32)


def _attn_call(q, k, v, *, batch, seq, window, dilation):
    width = q.shape[1] // dilation
    assert seq % (dilation * A_BLK) == 0 and window % dilation == 0
    length = seq // dilation
    tq = min(length, 4 * A_BLK)
    sub = lambda t: t.reshape(batch, length, dilation * width)
    o, lse = pl.pallas_call(
        functools.partial(_attn_kernel, tq=tq),
        grid=(batch, dilation, length // tq),
        in_specs=[
            _const_spec((2, A_BLK, 2 * A_BLK)),
            pl.BlockSpec((1, tq, width), lambda b, r, t: (b, t, r)),
            pl.BlockSpec((1, length, width), lambda b, r, t: (b, 0, r)),
            pl.BlockSpec((1, length, width), lambda b, r, t: (b, 0, r)),
        ],
        out_specs=[
            pl.BlockSpec((1, tq, width), lambda b, r, t: (b, t, r)),
            pl.BlockSpec((1, tq, LANES), lambda b, r, t: (b, t, r)),
        ],
        out_shape=[
            jax.ShapeDtypeStruct((batch, length, dilation * width), BF16),
            jax.ShapeDtypeStruct((batch, length, dilation * LANES), F32),
        ],
        compiler_params=_params("parallel", "parallel", "arbitrary"),
        name=f"attn_d{dilation}",
    )(_band_bias(window // dilation), sub(q), sub(k), sub(v))
    return (o.reshape(batch * length, dilation * width), lse.reshape(batch * length, dilation * LANES))


def _hgrn_kernel(lvl_ref, hglb_ref, gn_ref, q_ref, f_ref, i_ref, g_ref, y_ref, state_ref):
    tile, width = q_ref.shape
    heads = width // B_HEAD_DIM

    @pl.when(pl.program_id(1) == 0)
    def _():
        state_ref[...] = jnp.zeros_like(state_ref)

    hg = hglb_ref[...]
    ex = jnp.exp(hg - jnp.max(hg, axis=0, keepdims=True))
    lb_all = ex[0:1] / jnp.sum(ex, axis=0, keepdims=True)

    row = lax.broadcasted_iota(jnp.int32, (tile, B_HEAD_DIM), 0)
    lvl = lvl_ref[...]
    for h in range(heads):
        cols = pl.ds(h * B_HEAD_DIM, B_HEAD_DIM)
        lb = lb_all[:, h * B_HEAD_DIM:(h + 1) * B_HEAD_DIM]
        f = lb + (1.0 - lb) * jax.nn.sigmoid(f_ref[:, cols])
        q = q_ref[:, cols].astype(F32)
        kk = 1.0 - f
        v = i_ref[:, cols]
        pre = jnp.log(f)
        tot = pre
        scores = jnp.where(lvl == 0, _dot_nt(q.astype(BF16), kk.astype(BF16)), 0.0)
        m, level = 1, 1
        while m < tile:
            upper = (row & m) != 0
            dec = jnp.exp(jnp.where(upper, pre, tot - pre))
            ql = jnp.where(upper, q * dec, 0.0).astype(BF16)
            kl = jnp.where(upper, 0.0, kk * dec).astype(BF16)
            scores = jnp.where(lvl == level, _dot_nt(ql, kl), scores)
            below = pltpu.roll(tot, m, 0)
            above = pltpu.roll(tot, tile - m, 0)
            pre = pre + jnp.where(upper, below, 0.0)
            tot = tot + jnp.where(upper, below, above)
            m, level = 2 * m, level + 1
        st = state_ref[h]
        o = _dot_nt((q * jnp.exp(pre)).astype(BF16), st.astype(BF16)) + _dot(scores.astype(BF16), v)
        k_end = (kk * jnp.exp(tot - pre)).astype(BF16)
        state_ref[h] = st * jnp.exp(tot[0:1, :]) + _dot_tn(v, k_end)
        gate = g_ref[:, cols].astype(F32)
        y_ref[:, cols] = (_rms(o, gn_ref[...]) * (gate * jax.nn.sigmoid(gate))).astype(y_ref.dtype)


def _level_map(tile):
    t = np.arange(tile)[:, None]
    s = np.arange(tile)[None, :]
    x = np.maximum(t ^ s, 1)
    lv = np.floor(np.log2(x)).astype(np.int32) + 1
    return jnp.asarray(np.where(s < t, lv, np.where(s == t, 0, -1)), jnp.int32)


def _hgrn_call(qb, fb, ib, gb, hg_lb, g_norm, *, batch, seq):
    m, width = qb.shape
    tile = HG_TILE
    assert seq % tile == 0
    steps = seq // tile
    row = pl.BlockSpec((tile, width), lambda b, t: (b * steps + t, 0))
    return pl.pallas_call(
        _hgrn_kernel,
        grid=(batch, steps),
        in_specs=[_const_spec((tile, tile)), _const_spec(hg_lb.shape), _const_spec((1, B_HEAD_DIM)),
                  row, row, row, row],
        out_specs=row,
        out_shape=jax.ShapeDtypeStruct((m, width), BF16),
        scratch_shapes=[pltpu.VMEM((width // B_HEAD_DIM, B_HEAD_DIM, B_HEAD_DIM), F32)],
        compiler_params=_params("parallel", "arbitrary"),
        name="hgrn2",
    )(_level_map(tile), hg_lb, g_norm, qb, fb, ib, gb)


def _merge_kernel(x_ref, o1_ref, o2_ref, o3_ref, l1_ref, l2_ref, l3_ref, yb_ref, ga_ref, gg_ref,
                  ex_ref, wa_ref, wb_ref, wo_ref, out_ref, oslab_ref, lslab_ref, *, dilations):
    tm = x_ref.shape[0]
    slabs = oslab_ref.shape[0]

    def natural_lse(l_ref, d):
        if d == 1:
            return l_ref[...]
        for r in range(d):
            lslab_ref[pl.ds(r, tm // d, stride=d), :] = l_ref[:, pl.ds(r * LANES, LANES)]
        return lslab_ref[...]

    lses = [natural_lse(l, d) for l, d in zip((l1_ref, l2_ref, l3_ref), dilations)]
    top = jnp.maximum(jnp.maximum(lses[0], lses[1]), lses[2])
    ws = [jnp.exp(l - top) for l in lses]
    inv = 1.0 / (ws[0] + ws[1] + ws[2])
    ya = None
    for w, o_ref, d in zip(ws, (o1_ref, o2_ref, o3_ref), dilations):
        if d == 1:
            o = o_ref[...].astype(F32)
        else:
            for r in range(d):
                for c in range(slabs):
                    oslab_ref[c, pl.ds(r, tm // d, stride=d), :] = (
                        o_ref[:, pl.ds((r * slabs + c) * LANES, LANES)].astype(F32))
            o = jnp.concatenate([oslab_ref[c] for c in range(slabs)], axis=1)
        term = _dot((w * inv).astype(BF16), ex_ref[...]) * o
        ya = term if ya is None else ya + term
    a = _dot(ya.astype(BF16), wa_ref[...])
    b = _dot(yb_ref[...], wb_ref[...])
    merged = ga_ref[...].astype(F32) * a + gg_ref[...].astype(F32) * b
    out_ref[...] = x_ref[...] + _dot(merged.astype(BF16), wo_ref[...])


def _merge_call(x2, os_, lses, yb, ga, gg, wa, wb, wo, *, tm, dilations):
    m, d = x2.shape
    width = yb.shape[1]
    expand = np.zeros((LANES, width), np.float32)
    for h in range(width // A_HEAD_DIM):
        expand[h, h * A_HEAD_DIM:(h + 1) * A_HEAD_DIM] = 1.0
    row = lambda n: pl.BlockSpec((tm, n), lambda i: (i, 0))
    dil = lambda n: [pl.BlockSpec((tm // dl, dl * n), lambda i: (i, 0)) for dl in dilations]
    return pl.pallas_call(
        functools.partial(_merge_kernel, dilations=dilations),
        grid=(m // tm,),
        in_specs=[row(d)] + dil(width) + dil(LANES) + [row(width), row(d), row(d)]
        + [_const_spec((LANES, width)), _const_spec(wa.shape), _const_spec(wb.shape),
           _const_spec(wo.shape)],
        out_specs=row(d),
        out_shape=jax.ShapeDtypeStruct((m, d), F32),
        scratch_shapes=[pltpu.VMEM((width // LANES, tm, LANES), F32), pltpu.VMEM((tm, LANES), F32)],
        compiler_params=_params("parallel"),
        name="merge",
    )(x2, *os_, *lses, yb, ga, gg, jnp.asarray(expand, BF16), wa, wb, wo)


def _ffn_kernel(x_ref, p_ref, nf_ref, np_ref, nl_ref, wg_ref, wu_ref, wd_ref, wpe_ref, wpg_ref,
                out_ref, *, chunk):
    x = x_ref[...]
    h = _rms(x, nf_ref[...]).astype(BF16)
    hidden = wg_ref.shape[1]
    acc = x
    for c in range(0, hidden, chunk):
        g = _dot(h, wg_ref[:, c:c + chunk])
        u = _dot(h, wu_ref[:, c:c + chunk])
        acc = acc + _dot((g * jax.nn.sigmoid(g) * u).astype(BF16), wd_ref[c:c + chunk, :])
    hp = _rms(acc, np_ref[...]).astype(BF16)
    pe = _dot(p_ref[...].astype(BF16), wpe_ref[...])
    x3 = acc + pe * jax.nn.sigmoid(_dot(hp, wpg_ref[...]))
    out_ref[...] = _rms(x3, nl_ref[...])


def _ffn_call(x2, p2, nf, npl, nl, wg, wu, wd, wpe, wpg, *, tm, chunk):
    m, d = x2.shape
    row = lambda n: pl.BlockSpec((tm, n), lambda i: (i, 0))
    return pl.pallas_call(
        functools.partial(_ffn_kernel, chunk=chunk),
        grid=(m // tm,),
        in_specs=[row(d), row(p2.shape[1])] + [_const_spec((1, d))] * 3
        + [_const_spec(w.shape) for w in (wg, wu, wd, wpe, wpg)],
        out_specs=row(d),
        out_shape=jax.ShapeDtypeStruct((m, d), F32),
        compiler_params=_params("parallel"),
        name="ffn",
    )(x2, p2, nf, npl, nl, wg, wu, wd, wpe, wpg)


def kernel(x, p, norm_mix, w_in, hg_lb, hg_norm, w_a_up, w_b_up, w_out, norm_ffn, w_gu, w_down,
           norm_ple, w_pe, w_pg, norm_final):
    batch, seq, d = x.shape
    depth = p.shape[0]
    assert depth == 1 and hg_lb.shape[0] == 2
    m = batch * seq
    tm = 512
    hidden = w_down.shape[1]
    bf = lambda w: w.astype(BF16)
    vec = lambda g: g.reshape(1, -1).astype(F32)

    dilations = tuple(dl for _, dl in A_PATTERNS)
    x2 = x.reshape(m, d)
    outs = _proj_call(x2, vec(norm_mix[0]), bf(w_in[0]), tm=tm, dilations=dilations)
    nd = len(dilations)
    qs, ks, vs = outs[:nd], outs[nd:2 * nd], outs[2 * nd:3 * nd]
    qb, fb, ib, gb, ga, gg = outs[3 * nd:]
    os_, lses = [], []
    for g, (window, dilation) in enumerate(A_PATTERNS):
        o, lse = _attn_call(qs[g], ks[g], vs[g], batch=batch, seq=seq, window=window, dilation=dilation)
        os_.append(o)
        lses.append(lse)
    yb = _hgrn_call(qb, fb, ib, gb, hg_lb.astype(F32), vec(hg_norm[0]), batch=batch, seq=seq)
    x2 = _merge_call(x2, os_, lses, yb, ga, gg, bf(w_a_up[0]), bf(w_b_up[0]), bf(w_out[0]), tm=tm,
                     dilations=dilations)
    out = _ffn_call(x2, p[0].reshape(m, -1), vec(norm_ffn[0]), vec(norm_ple[0]), vec(norm_final),
                    bf(w_gu[0, :, :hidden]), bf(w_gu[0, :, hidden:]), bf(w_down[0]), bf(w_pe[0]),
                    bf(w_pg[0]), tm=tm, chunk=256)
    return out.reshape(batch, seq, d)
```

```python
import functools

import jax
import jax.numpy as jnp
import numpy as np
from jax import lax
from jax.experimental import pallas as pl
from jax.experimental.pallas import tpu as pltpu

F32 = jnp.float32
BF16 = jnp.bfloat16

LANES = 128
EPS = 1e-6
NEG = -1e30

A_HEAD_DIM = 64
A_PATTERNS = ((128, 1), (512, 4), (2048, 16))
A_BLK = 128
B_HEAD_DIM = 128
HG_TILE = 256

VMEM_LIMIT = 56 * 1024 * 1024


def _params(*sem):
    return pltpu.CompilerParams(dimension_semantics=sem, vmem_limit_bytes=VMEM_LIMIT)


def _const_spec(shape):
    nd = len(shape)
    return pl.BlockSpec(shape, lambda *_: (0,) * nd, pipeline_mode=pl.Buffered(1))


def _dot(a, b):
    return jnp.dot(a, b, preferred_element_type=F32)


def _dot_nt(a, b):
    return lax.dot_general(a, b, (((1,), (1,)), ((), ())), preferred_element_type=F32)


def _dot_tn(a, b):
    return lax.dot_general(a, b, (((0,), (0,)), ((), ())), preferred_element_type=F32)


def _rms(x, g):
    return x * lax.rsqrt(jnp.mean(x * x, axis=-1, keepdims=True) + EPS) * g


SUB = 8


def _hgrn_tile(q, fgate, v, gate, lb, gn, lvl, state_ref, head):
    tile = q.shape[0]
    groups, half = tile // SUB, tile // 2
    f = lb + (1.0 - lb) * jax.nn.sigmoid(fgate)
    kk = 1.0 - f
    g3 = lambda t: t.reshape(groups, SUB, t.shape[-1])
    q3, k3 = g3(q), g3(kk)

    sub = lax.broadcasted_iota(jnp.int32, (1, SUB, q.shape[-1]), 1)
    pre = tot = g3(jnp.log2(f))
    ys = []
    m = 1
    while m < SUB:
        upper = (sub & m) != 0
        y3 = jnp.where(upper, q3, k3) * jnp.exp2(jnp.where(upper, pre, tot - pre))
        ys.append(y3.reshape(tile, -1).astype(BF16))
        below = pltpu.roll(tot, m, 1)
        above = pltpu.roll(tot, SUB - m, 1)
        pre = pre + jnp.where(upper, below, 0.0)
        tot = tot + jnp.where(upper, below, above)
        m *= 2

    ends = [tot[0]]
    for g in range(1, groups):
        ends.append(ends[-1] + tot[g])
    b = jnp.concatenate([pre[0]] + [pre[g] + ends[g - 1] for g in range(1, groups)], axis=0)
    rep = lambda t, n: jnp.concatenate([t] * n, axis=0)
    while m < tile:
        parts = []
        for lo in range(0, tile, 2 * m):
            mid = rep(ends[(lo + m) // SUB - 1], m // SUB)
            parts.append(kk[lo:lo + m] * jnp.exp2(mid - b[lo:lo + m]))
            parts.append(q[lo + m:lo + 2 * m] * jnp.exp2(b[lo + m:lo + 2 * m] - mid))
        ys.append(jnp.concatenate(parts, axis=0).astype(BF16))
        m *= 2

    qb16, kb16 = q.astype(BF16), kk.astype(BF16)
    diag = []
    for r0 in (0, half):
        rows = slice(r0, r0 + half)
        sc = jnp.where(lvl == 0, _dot_nt(qb16[rows], kb16[rows]), 0.0)
        for level, y in enumerate(ys[:-1], start=1):
            sc = jnp.where(lvl == level, _dot_nt(y[rows], y[rows]), sc)
        diag.append(sc.astype(BF16))
    cross = _dot_nt(ys[-1][half:], ys[-1][:half]).astype(BF16)

    st = state_ref[head]
    whole = rep(ends[-1], groups)
    inter = _dot_nt((q * jnp.exp2(b)).astype(BF16), st.astype(BF16))
    o = inter + jnp.concatenate(
        [_dot(diag[0], v[:half]), _dot(jnp.concatenate([cross, diag[1]], axis=1), v)], axis=0)
    k_end = (kk * jnp.exp2(whole - b)).astype(BF16)
    state_ref[head] = st * jnp.exp2(ends[-1][0:1, :]) + _dot_tn(v, k_end)
    return _rms(o, gn) * (gate * jax.nn.sigmoid(gate))


def _front_kernel(lvl_ref, hglb_ref, gn_ref, x_ref, g_ref, w_ref, *refs, width, dilations):
    nd = len(dilations)
    attn_refs = refs[:3 * nd]
    yb_ref, ga_ref, gg_ref, hx_ref, slab_ref, state_ref = refs[3 * nd:]
    tm = x_ref.shape[0]
    heads = width // B_HEAD_DIM

    @pl.when(pl.program_id(1) == 0)
    def _():
        state_ref[...] = jnp.zeros_like(state_ref)

    hx_ref[...] = _rms(x_ref[...], g_ref[...]).astype(BF16)
    half = tm // 2
    slabs = width // LANES

    def proj(t0, rows, c0, n):
        return _dot(hx_ref[pl.ds(t0, rows), :], w_ref[:, c0:c0 + n])

    def attn_job(a, t0):
        y = proj(t0, half, a * width, width)
        if a == 0:
            y = y * (A_HEAD_DIM ** -0.5)
        for c in range(slabs):
            slab_ref[c, pl.ds(t0, half), :] = y[:, c * LANES:(c + 1) * LANES]
        for out_ref, d in zip(attn_refs[a * nd:(a + 1) * nd], dilations):
            if d == 1:
                out_ref[pl.ds(t0, half), :] = y.astype(BF16)
                continue
            for r in range(d):
                for c in range(slabs):
                    out_ref[pl.ds(t0 // d, half // d), pl.ds((r * slabs + c) * LANES, LANES)] = (
                        slab_ref[c, pl.ds(t0 + r, half // d, stride=d), :].astype(BF16))

    def gate_job(out_ref, c0, j, t0):
        y = proj(t0, half, c0 + j * width, width)
        out_ref[pl.ds(t0, half), pl.ds(j * width, width)] = jax.nn.sigmoid(y).astype(BF16)

    jobs = [functools.partial(attn_job, a, t0) for a in range(3) for t0 in (0, half)]
    jobs += [functools.partial(gate_job, ref, c0, j, t0)
             for ref, c0 in ((ga_ref, 7 * width), (gg_ref, 9 * width)) for j in range(2)
             for t0 in (0, half)]

    hg = hglb_ref[...]
    ex = jnp.exp(hg - jnp.max(hg, axis=0, keepdims=True))
    lb_all = ex[0:1] / jnp.sum(ex, axis=0, keepdims=True)
    lvl = lvl_ref[...]
    units = heads * (tm // HG_TILE)
    done = 0
    head_proj = lambda hd: proj(0, tm, 3 * width + 4 * hd * B_HEAD_DIM, 4 * B_HEAD_DIM)
    tiles = list(range(0, tm, HG_TILE))
    hp_next = head_proj(0)
    for hd in range(heads):
        c0 = hd * B_HEAD_DIM
        hp = hp_next
        for t0 in tiles:
            if t0 == tiles[-1] and hd + 1 < heads:
                hp_next = head_proj(hd + 1)
            part = lambda j: hp[t0:t0 + HG_TILE, j * B_HEAD_DIM:(j + 1) * B_HEAD_DIM]
            y = _hgrn_tile(part(0), part(1), part(2).astype(BF16), part(3),
                           lb_all[:, c0:c0 + B_HEAD_DIM], gn_ref[...], lvl, state_ref, hd)
            yb_ref[pl.ds(t0, HG_TILE), pl.ds(c0, B_HEAD_DIM)] = y.astype(yb_ref.dtype)
            done += 1
            while jobs and len(jobs) * units > (units - done) * 14:
                jobs.pop(0)()
    for job in jobs:
        job()


def _level_map(tile):
    t = np.arange(tile)[:, None]
    s = np.arange(tile)[None, :]
    x = np.maximum(t ^ s, 1)
    lv = np.floor(np.log2(x)).astype(np.int32) + 1
    return jnp.asarray(np.where(s < t, lv, np.where(s == t, 0, -1)), jnp.int32)


def _front_call(x2, g, w, hg_lb, g_norm, *, batch, seq, tm, dilations):
    m, d = x2.shape
    width = d // 2
    assert seq % tm == 0 and tm % HG_TILE == 0
    steps = seq // tm
    row = lambda n: pl.BlockSpec((tm, n), lambda b, t: (b * steps + t, 0))
    sds = lambda n, dt: jax.ShapeDtypeStruct((m, n), dt)
    dil_specs = [pl.BlockSpec((tm // dl, dl * width), lambda b, t: (b * steps + t, 0))
                 for dl in dilations] * 3
    dil_shapes = [jax.ShapeDtypeStruct((m // dl, dl * width), BF16) for dl in dilations] * 3
    return pl.pallas_call(
        functools.partial(_front_kernel, width=width, dilations=dilations),
        grid=(batch, steps),
        in_specs=[_const_spec((HG_TILE // 2, HG_TILE // 2)), _const_spec(hg_lb.shape),
                  _const_spec((1, B_HEAD_DIM)), row(d), _const_spec((1, d)), _const_spec(w.shape)],
        out_specs=dil_specs + [row(width), row(d), row(d)],
        out_shape=dil_shapes + [sds(width, BF16), sds(d, BF16), sds(d, BF16)],
        scratch_shapes=[pltpu.VMEM((tm, d), BF16),
                        pltpu.VMEM((width // LANES, tm, LANES), F32),
                        pltpu.VMEM((width // B_HEAD_DIM, B_HEAD_DIM, B_HEAD_DIM), F32)],
        compiler_params=_params("parallel", "arbitrary"),
        name="front",
    )(_level_map(HG_TILE // 2), hg_lb, g_norm, x2, g, _head_major(w, width))


def _head_major(w, width):
    heads = width // B_HEAD_DIM
    hg = w[:, 3 * width:7 * width].reshape(w.shape[0], 4, heads, B_HEAD_DIM)
    hg = hg.transpose(0, 2, 1, 3).reshape(w.shape[0], 4 * width)
    return jnp.concatenate([w[:, :3 * width], hg, w[:, 7 * width:]], axis=1)


def _attn_kernel(bias_ref, q_ref, k_ref, v_ref, o_ref, lse_ref, *, tq):
    width = q_ref.shape[2]
    pairs = width // LANES
    lane = lax.broadcasted_iota(jnp.int32, (A_BLK, LANES), 1)
    low_half = lane < A_HEAD_DIM
    ones = jnp.ones((2 * A_BLK, LANES), BF16)
    for qi in range(tq // A_BLK):
        n = pl.program_id(2) * (tq // A_BLK) + qi
        first = n == 0
        start = pl.multiple_of(jnp.maximum(n - 1, 0) * A_BLK, A_BLK)
        bias = bias_ref[jnp.where(first, 1, 0)]
        rows = pl.ds(qi * A_BLK, A_BLK)
        lse_tile = jnp.zeros((A_BLK, LANES), F32)
        for p in range(pairs):
            cols = pl.ds(p * LANES, LANES)
            q_pair = q_ref[0, rows, cols]
            k_band = k_ref[0, pl.ds(start, 2 * A_BLK), cols]
            v_cat = jnp.concatenate([v_ref[0, pl.ds(start, 2 * A_BLK), cols], ones], axis=1)
            halves = []
            for hh in range(2):
                mine = low_half if hh == 0 else jnp.logical_not(low_half)
                s = _dot_nt(jnp.where(mine, q_pair, jnp.zeros_like(q_pair)), k_band) + bias
                mx = jnp.max(s, axis=-1, keepdims=True)
                e = jnp.exp(s - mx).astype(BF16)
                r = _dot(e, v_cat)
                den = r[:, LANES:]
                halves.append(r[:, :LANES] / den)
                lse_tile = jnp.where(lane == 2 * p + hh, mx + jnp.log(den), lse_tile)
            o_ref[0, rows, cols] = jnp.where(low_half, halves[0], halves[1]).astype(o_ref.dtype)
        lse_ref[0, rows, :] = lse_tile


def _band_bias(span):
    a = np.arange(A_BLK)[:, None]
    c = np.arange(2 * A_BLK)[None, :]
    rel = a + A_BLK - c
    normal = (rel >= 0) & (rel <= span)
    rel0 = a - c
    first = (rel0 >= 0) & (rel0 <= span)
    return jnp.asarray(np.where(np.stack([normal, first]), 0.0, NEG), F32)


def _attn_call(q, k, v, *, batch, seq, window, dilation):
    width = q.shape[1] // dilation
    assert seq % (dilation * A_BLK) == 0 and window % dilation == 0
    length = seq // dilation
    tq = min(length, 4 * A_BLK)
    sub = lambda t: t.reshape(batch, length, dilation * width)
    o, lse = pl.pallas_call(
        functools.partial(_attn_kernel, tq=tq),
        grid=(batch, dilation, length // tq),
        in_specs=[
            _const_spec((2, A_BLK, 2 * A_BLK)),
            pl.BlockSpec((1, tq, width), lambda b, r, t: (b, t, r)),
            pl.BlockSpec((1, length, width), lambda b, r, t: (b, 0, r)),
            pl.BlockSpec((1, length, width), lambda b, r, t: (b, 0, r)),
        ],
        out_specs=[
            pl.BlockSpec((1, tq, width), lambda b, r, t: (b, t, r)),
            pl.BlockSpec((1, tq, LANES), lambda b, r, t: (b, t, r)),
        ],
        out_shape=[
            jax.ShapeDtypeStruct((batch, length, dilation * width), BF16),
            jax.ShapeDtypeStruct((batch, length, dilation * LANES), F32),
        ],
        compiler_params=_params("parallel", "parallel", "arbitrary"),
        name=f"attn_d{dilation}",
    )(_band_bias(window // dilation), sub(q), sub(k), sub(v))
    return (o.reshape(batch * length, dilation * width), lse.reshape(batch * length, dilation * LANES))


def _merge_kernel(x_ref, o1_ref, o2_ref, o3_ref, l1_ref, l2_ref, l3_ref, yb_ref, ga_ref, gg_ref,
                  ex_ref, wa_ref, wb_ref, wo_ref, out_ref, oslab_ref, lslab_ref, *, dilations):
    tm = x_ref.shape[0]
    slabs = oslab_ref.shape[0]

    def natural_lse(l_ref, d):
        if d == 1:
            return l_ref[...]
        for r in range(d):
            lslab_ref[pl.ds(r, tm // d, stride=d), :] = l_ref[:, pl.ds(r * LANES, LANES)]
        return lslab_ref[...]

    lses = [natural_lse(l, d) for l, d in zip((l1_ref, l2_ref, l3_ref), dilations)]
    top = jnp.maximum(jnp.maximum(lses[0], lses[1]), lses[2])
    ws = [jnp.exp(l - top) for l in lses]
    inv = 1.0 / (ws[0] + ws[1] + ws[2])
    ya = None
    for w, o_ref, d in zip(ws, (o1_ref, o2_ref, o3_ref), dilations):
        if d == 1:
            o = o_ref[...].astype(F32)
        else:
            for r in range(d):
                for c in range(slabs):
                    oslab_ref[c, pl.ds(r, tm // d, stride=d), :] = (
                        o_ref[:, pl.ds((r * slabs + c) * LANES, LANES)].astype(F32))
            o = jnp.concatenate([oslab_ref[c] for c in range(slabs)], axis=1)
        term = _dot((w * inv).astype(BF16), ex_ref[...]) * o
        ya = term if ya is None else ya + term
    a = _dot(ya.astype(BF16), wa_ref[...])
    b = _dot(yb_ref[...], wb_ref[...])
    merged = ga_ref[...].astype(F32) * a + gg_ref[...].astype(F32) * b
    out_ref[...] = x_ref[...] + _dot(merged.astype(BF16), wo_ref[...])


def _merge_call(x2, os_, lses, yb, ga, gg, wa, wb, wo, *, tm, dilations):
    m, d = x2.shape
    width = yb.shape[1]
    expand = np.zeros((LANES, width), np.float32)
    for h in range(width // A_HEAD_DIM):
        expand[h, h * A_HEAD_DIM:(h + 1) * A_HEAD_DIM] = 1.0
    row = lambda n: pl.BlockSpec((tm, n), lambda i: (i, 0))
    dil = lambda n: [pl.BlockSpec((tm // dl, dl * n), lambda i: (i, 0)) for dl in dilations]
    return pl.pallas_call(
        functools.partial(_merge_kernel, dilations=dilations),
        grid=(m // tm,),
        in_specs=[row(d)] + dil(width) + dil(LANES) + [row(width), row(d), row(d)]
        + [_const_spec((LANES, width)), _const_spec(wa.shape), _const_spec(wb.shape),
           _const_spec(wo.shape)],
        out_specs=row(d),
        out_shape=jax.ShapeDtypeStruct((m, d), F32),
        scratch_shapes=[pltpu.VMEM((width // LANES, tm, LANES), F32), pltpu.VMEM((tm, LANES), F32)],
        compiler_params=_params("parallel"),
        name="merge",
    )(x2, *os_, *lses, yb, ga, gg, jnp.asarray(expand, BF16), wa, wb, wo)


def _ffn_kernel(x_ref, p_ref, nf_ref, np_ref, nl_ref, wg_ref, wu_ref, wd_ref, wpe_ref, wpg_ref,
                out_ref, *, chunk):
    x = x_ref[...]
    h = _rms(x, nf_ref[...]).astype(BF16)
    hidden = wg_ref.shape[1]
    acc = x
    for c in range(0, hidden, chunk):
        g = _dot(h, wg_ref[:, c:c + chunk])
        u = _dot(h, wu_ref[:, c:c + chunk])
        acc = acc + _dot((g * jax.nn.sigmoid(g) * u).astype(BF16), wd_ref[c:c + chunk, :])
    hp = _rms(acc, np_ref[...]).astype(BF16)
    pe = _dot(p_ref[...].astype(BF16), wpe_ref[...])
    x3 = acc + pe * jax.nn.sigmoid(_dot(hp, wpg_ref[...]))
    out_ref[...] = _rms(x3, nl_ref[...])


def _ffn_call(x2, p2, nf, npl, nl, wg, wu, wd, wpe, wpg, *, tm, chunk):
    m, d = x2.shape
    row = lambda n: pl.BlockSpec((tm, n), lambda i: (i, 0))
    return pl.pallas_call(
        functools.partial(_ffn_kernel, chunk=chunk),
        grid=(m // tm,),
        in_specs=[row(d), row(p2.shape[1])] + [_const_spec((1, d))] * 3
        + [_const_spec(w.shape) for w in (wg, wu, wd, wpe, wpg)],
        out_specs=row(d),
        out_shape=jax.ShapeDtypeStruct((m, d), F32),
        compiler_params=_params("parallel"),
        name="ffn",
    )(x2, p2, nf, npl, nl, wg, wu, wd, wpe, wpg)


def kernel(x, p, norm_mix, w_in, hg_lb, hg_norm, w_a_up, w_b_up, w_out, norm_ffn, w_gu, w_down,
           norm_ple, w_pe, w_pg, norm_final):
    batch, seq, d = x.shape
    depth = p.shape[0]
    assert depth == 1 and hg_lb.shape[0] == 2
    m = batch * seq
    tm = 512
    hidden = w_down.shape[1]
    bf = lambda w: w.astype(BF16)
    vec = lambda g: g.reshape(1, -1).astype(F32)

    dilations = tuple(dl for _, dl in A_PATTERNS)
    x2 = x.reshape(m, d)
    outs = _front_call(x2, vec(norm_mix[0]), bf(w_in[0]), hg_lb.astype(F32), vec(hg_norm[0]),
                       batch=batch, seq=seq, tm=tm, dilations=dilations)
    nd = len(dilations)
    qs, ks, vs = outs[:nd], outs[nd:2 * nd], outs[2 * nd:3 * nd]
    yb, ga, gg = outs[3 * nd:]
    os_, lses = [], []
    for g, (window, dilation) in enumerate(A_PATTERNS):
        o, lse = _attn_call(qs[g], ks[g], vs[g], batch=batch, seq=seq, window=window, dilation=dilation)
        os_.append(o)
        lses.append(lse)
    x2 = _merge_call(x2, os_, lses, yb, ga, gg, bf(w_a_up[0]), bf(w_b_up[0]), bf(w_out[0]), tm=tm,
                     dilations=dilations)
    out = _ffn_call(x2, p[0].reshape(m, -1), vec(norm_ffn[0]), vec(norm_ple[0]), vec(norm_final),
                    bf(w_gu[0, :, :hidden]), bf(w_gu[0, :, hidden:]), bf(w_down[0]), bf(w_pe[0]),
                    bf(w_pg[0]), tm=tm, chunk=256)
    return out.reshape(batch, seq, d)
```

```python
import functools

import jax
import jax.numpy as jnp
import numpy as np
from jax import lax
from jax.experimental import pallas as pl
from jax.experimental.pallas import tpu as pltpu

F32 = jnp.float32
BF16 = jnp.bfloat16

LANES = 128
EPS = 1e-6
NEG = -1e30
LOG2E = 1.4426950408889634

A_HEAD_DIM = 64
A_PATTERNS = ((128, 1), (512, 4), (2048, 16))
A_BLK = 128
ATTN_UNITS = 8
B_HEAD_DIM = 128
HG_TILE = 256

VMEM_LIMIT = 56 * 1024 * 1024


def _params(*sem):
    return pltpu.CompilerParams(dimension_semantics=sem, vmem_limit_bytes=VMEM_LIMIT)


def _const_spec(shape):
    nd = len(shape)
    return pl.BlockSpec(shape, lambda *_: (0,) * nd, pipeline_mode=pl.Buffered(1))


def _dot(a, b):
    return jnp.dot(a, b, preferred_element_type=F32)


def _dot_nt(a, b):
    return lax.dot_general(a, b, (((1,), (1,)), ((), ())), preferred_element_type=F32)


def _dot_tn(a, b):
    return lax.dot_general(a, b, (((0,), (0,)), ((), ())), preferred_element_type=F32)


def _rms(x, g):
    return x * lax.rsqrt(jnp.mean(x * x, axis=-1, keepdims=True) + EPS) * g


SUB = 8


def _hgrn_tile(q, fgate, v, gate, lb, gn, lvl, state_ref, head):
    tile = q.shape[0]
    groups, half = tile // SUB, tile // 2
    f = lb + (1.0 - lb) * jax.nn.sigmoid(fgate)
    kk = 1.0 - f
    g3 = lambda t: t.reshape(groups, SUB, t.shape[-1])
    q3, k3 = g3(q), g3(kk)

    sub = lax.broadcasted_iota(jnp.int32, (1, SUB, q.shape[-1]), 1)
    pre = tot = g3(jnp.log2(f))
    ys = []
    m = 1
    while m < SUB:
        upper = (sub & m) != 0
        y3 = jnp.where(upper, q3, k3) * jnp.exp2(jnp.where(upper, pre, tot - pre))
        ys.append(y3.reshape(tile, -1).astype(BF16))
        below = pltpu.roll(tot, m, 1)
        above = pltpu.roll(tot, SUB - m, 1)
        pre = pre + jnp.where(upper, below, 0.0)
        tot = tot + jnp.where(upper, below, above)
        m *= 2

    ends = [tot[0]]
    for g in range(1, groups):
        ends.append(ends[-1] + tot[g])
    b = jnp.concatenate([pre[0]] + [pre[g] + ends[g - 1] for g in range(1, groups)], axis=0)
    rep = lambda t, n: jnp.concatenate([t] * n, axis=0)
    while m < tile:
        parts = []
        for lo in range(0, tile, 2 * m):
            mid = rep(ends[(lo + m) // SUB - 1], m // SUB)
            parts.append(kk[lo:lo + m] * jnp.exp2(mid - b[lo:lo + m]))
            parts.append(q[lo + m:lo + 2 * m] * jnp.exp2(b[lo + m:lo + 2 * m] - mid))
        ys.append(jnp.concatenate(parts, axis=0).astype(BF16))
        m *= 2

    qb16, kb16 = q.astype(BF16), kk.astype(BF16)
    diag = []
    for r0 in (0, half):
        rows = slice(r0, r0 + half)
        sc = jnp.where(lvl == 0, _dot_nt(qb16[rows], kb16[rows]), 0.0)
        for level, y in enumerate(ys[:-1], start=1):
            sc = jnp.where(lvl == level, _dot_nt(y[rows], y[rows]), sc)
        diag.append(sc.astype(BF16))
    cross = _dot_nt(ys[-1][half:], ys[-1][:half]).astype(BF16)

    st = state_ref[head]
    whole = rep(ends[-1], groups)
    inter = _dot_nt((q * jnp.exp2(b)).astype(BF16), st.astype(BF16))
    o = inter + jnp.concatenate(
        [_dot(diag[0], v[:half]), _dot(jnp.concatenate([cross, diag[1]], axis=1), v)], axis=0)
    k_end = (kk * jnp.exp2(whole - b)).astype(BF16)
    state_ref[head] = st * jnp.exp2(ends[-1][0:1, :]) + _dot_tn(v, k_end)
    return _rms(o, gn) * (gate * jax.nn.sigmoid(gate))


def _front_kernel(lvl_ref, hglb_ref, gn_ref, x_ref, g_ref, w_ref, *refs, width, dilations):
    nd = len(dilations)
    attn_refs = refs[:3 * nd]
    yb_ref, ga_ref, gg_ref, hx_ref, slab_ref, state_ref = refs[3 * nd:]
    tm = x_ref.shape[0]
    heads = width // B_HEAD_DIM

    @pl.when(pl.program_id(1) == 0)
    def _():
        state_ref[...] = jnp.zeros_like(state_ref)

    hx_ref[...] = _rms(x_ref[...], g_ref[...]).astype(BF16)
    half = tm // 2
    slabs = width // LANES

    def proj(t0, rows, c0, n):
        return _dot(hx_ref[pl.ds(t0, rows), :], w_ref[:, c0:c0 + n])

    def attn_job(a, t0):
        y = proj(t0, half, a * width, width)
        if a == 0:
            y = y * (A_HEAD_DIM ** -0.5 * LOG2E)
        for c in range(slabs):
            slab_ref[c, pl.ds(t0, half), :] = y[:, c * LANES:(c + 1) * LANES]
        for out_ref, d in zip(attn_refs[a * nd:(a + 1) * nd], dilations):
            if d == 1:
                out_ref[pl.ds(t0, half), :] = y.astype(BF16)
                continue
            for r in range(d):
                for c in range(slabs):
                    out_ref[pl.ds(t0 // d, half // d), pl.ds((r * slabs + c) * LANES, LANES)] = (
                        slab_ref[c, pl.ds(t0 + r, half // d, stride=d), :].astype(BF16))

    def gate_job(out_ref, c0, j, t0):
        y = proj(t0, half, c0 + j * width, width)
        out_ref[pl.ds(t0, half), pl.ds(j * width, width)] = jax.nn.sigmoid(y).astype(BF16)

    jobs = [functools.partial(attn_job, a, t0) for a in range(3) for t0 in (0, half)]
    jobs += [functools.partial(gate_job, ref, c0, j, t0)
             for ref, c0 in ((ga_ref, 7 * width), (gg_ref, 9 * width)) for j in range(2)
             for t0 in (0, half)]

    hg = hglb_ref[...]
    ex = jnp.exp(hg - jnp.max(hg, axis=0, keepdims=True))
    lb_all = ex[0:1] / jnp.sum(ex, axis=0, keepdims=True)
    lvl = lvl_ref[...]
    units = heads * (tm // HG_TILE)
    done = 0
    head_proj = lambda hd: proj(0, tm, 3 * width + 4 * hd * B_HEAD_DIM, 4 * B_HEAD_DIM)
    tiles = list(range(0, tm, HG_TILE))
    hp_next = head_proj(0)
    for hd in range(heads):
        c0 = hd * B_HEAD_DIM
        hp = hp_next
        for t0 in tiles:
            if t0 == tiles[-1] and hd + 1 < heads:
                hp_next = head_proj(hd + 1)
            part = lambda j: hp[t0:t0 + HG_TILE, j * B_HEAD_DIM:(j + 1) * B_HEAD_DIM]
            y = _hgrn_tile(part(0), part(1), part(2).astype(BF16), part(3),
                           lb_all[:, c0:c0 + B_HEAD_DIM], gn_ref[...], lvl, state_ref, hd)
            yb_ref[pl.ds(t0, HG_TILE), pl.ds(c0, B_HEAD_DIM)] = y.astype(yb_ref.dtype)
            done += 1
            while jobs and len(jobs) * units > (units - done) * 14:
                jobs.pop(0)()
    for job in jobs:
        job()


def _level_map(tile):
    t = np.arange(tile)[:, None]
    s = np.arange(tile)[None, :]
    x = np.maximum(t ^ s, 1)
    lv = np.floor(np.log2(x)).astype(np.int32) + 1
    return jnp.asarray(np.where(s < t, lv, np.where(s == t, 0, -1)), jnp.int32)


def _front_call(x2, g, w, hg_lb, g_norm, *, batch, seq, tm, dilations):
    m, d = x2.shape
    width = d // 2
    assert seq % tm == 0 and tm % HG_TILE == 0
    steps = seq // tm
    row = lambda n: pl.BlockSpec((tm, n), lambda b, t: (b * steps + t, 0))
    sds = lambda n, dt: jax.ShapeDtypeStruct((m, n), dt)
    dil_specs = [pl.BlockSpec((tm // dl, dl * width), lambda b, t: (b * steps + t, 0))
                 for dl in dilations] * 3
    dil_shapes = [jax.ShapeDtypeStruct((m // dl, dl * width), BF16) for dl in dilations] * 3
    return pl.pallas_call(
        functools.partial(_front_kernel, width=width, dilations=dilations),
        grid=(batch, steps),
        in_specs=[_const_spec((HG_TILE // 2, HG_TILE // 2)), _const_spec(hg_lb.shape),
                  _const_spec((1, B_HEAD_DIM)), row(d), _const_spec((1, d)), _const_spec(w.shape)],
        out_specs=dil_specs + [row(width), row(d), row(d)],
        out_shape=dil_shapes + [sds(width, BF16), sds(d, BF16), sds(d, BF16)],
        scratch_shapes=[pltpu.VMEM((tm, d), BF16),
                        pltpu.VMEM((width // LANES, tm, LANES), F32),
                        pltpu.VMEM((width // B_HEAD_DIM, B_HEAD_DIM, B_HEAD_DIM), F32)],
        compiler_params=_params("parallel", "arbitrary"),
        name="front",
    )(_level_map(HG_TILE // 2), hg_lb, g_norm, x2, g, _head_major(w, width))


def _head_major(w, width):
    heads = width // B_HEAD_DIM
    hg = w[:, 3 * width:7 * width].reshape(w.shape[0], 4, heads, B_HEAD_DIM)
    hg = hg.transpose(0, 2, 1, 3).reshape(w.shape[0], 4 * width)
    return jnp.concatenate([w[:, :3 * width], hg, w[:, 7 * width:]], axis=1)


def _attn_kernel(bias_ref, q_ref, k_ref, v_ref, o_ref, lse_ref, *, tq, rblk):
    width = q_ref.shape[2] // rblk
    pairs = width // LANES
    lane = lax.broadcasted_iota(jnp.int32, (A_BLK, LANES), 1)
    low_half = lane < A_HEAD_DIM
    ones = jnp.ones((2 * A_BLK, LANES), BF16)
    for rr in range(rblk):
        for qi in range(tq // A_BLK):
            n = pl.program_id(2) * (tq // A_BLK) + qi
            start = pl.multiple_of(jnp.maximum(n - 1, 0) * A_BLK, A_BLK)
            bias = bias_ref[jnp.where(n == 0, 1, 0)]
            rows = pl.ds(qi * A_BLK, A_BLK)
            mx_tile = jnp.zeros((A_BLK, LANES), F32)
            den_tile = jnp.ones((A_BLK, LANES), F32)
            for p in range(pairs):
                cols = pl.ds(rr * width + p * LANES, LANES)
                q_pair = q_ref[0, rows, cols]
                k_band = k_ref[0, pl.ds(start, 2 * A_BLK), cols]
                v_cat = jnp.concatenate([v_ref[0, pl.ds(start, 2 * A_BLK), cols], ones], axis=1)
                nums, dens = [], []
                for hh in range(2):
                    mine = low_half if hh == 0 else jnp.logical_not(low_half)
                    s = _dot_nt(jnp.where(mine, q_pair, jnp.zeros_like(q_pair)), k_band) + bias
                    mx = jnp.max(s, axis=-1, keepdims=True)
                    r = _dot(jnp.exp2(s - mx).astype(BF16), v_cat)
                    nums.append(r[:, :LANES])
                    dens.append(r[:, LANES:])
                    mx_tile = jnp.where(lane == 2 * p + hh, mx, mx_tile)
                    den_tile = jnp.where(lane == 2 * p + hh, dens[-1], den_tile)
                o_ref[0, rows, cols] = (jnp.where(low_half, nums[0], nums[1])
                                        / jnp.where(low_half, dens[0], dens[1])).astype(o_ref.dtype)
            lse_ref[0, rows, pl.ds(rr * LANES, LANES)] = mx_tile + jnp.log2(den_tile)


def _band_bias(span):
    a = np.arange(A_BLK)[:, None]
    c = np.arange(2 * A_BLK)[None, :]
    rel = a + A_BLK - c
    normal = (rel >= 0) & (rel <= span)
    rel0 = a - c
    first = (rel0 >= 0) & (rel0 <= span)
    return jnp.asarray(np.where(np.stack([normal, first]), 0.0, NEG), F32)


def _attn_call(q, k, v, *, batch, seq, window, dilation):
    width = q.shape[1] // dilation
    assert seq % (dilation * A_BLK) == 0 and window % dilation == 0
    length = seq // dilation
    tq = min(length, ATTN_UNITS * A_BLK)
    rblk = min(dilation, ATTN_UNITS * A_BLK // tq)
    sub = lambda t: t.reshape(batch, length, dilation * width)
    o, lse = pl.pallas_call(
        functools.partial(_attn_kernel, tq=tq, rblk=rblk),
        grid=(batch, dilation // rblk, length // tq),
        in_specs=[
            _const_spec((2, A_BLK, 2 * A_BLK)),
            pl.BlockSpec((1, tq, rblk * width), lambda b, r, t: (b, t, r)),
            pl.BlockSpec((1, length, rblk * width), lambda b, r, t: (b, 0, r)),
            pl.BlockSpec((1, length, rblk * width), lambda b, r, t: (b, 0, r)),
        ],
        out_specs=[
            pl.BlockSpec((1, tq, rblk * width), lambda b, r, t: (b, t, r)),
            pl.BlockSpec((1, tq, rblk * LANES), lambda b, r, t: (b, t, r)),
        ],
        out_shape=[
            jax.ShapeDtypeStruct((batch, length, dilation * width), BF16),
            jax.ShapeDtypeStruct((batch, length, dilation * LANES), F32),
        ],
        compiler_params=_params("parallel", "parallel", "arbitrary"),
        name=f"attn_d{dilation}",
    )(_band_bias(window // dilation), sub(q), sub(k), sub(v))
    return (o.reshape(batch * length, dilation * width), lse.reshape(batch * length, dilation * LANES))


def _merge_kernel(x_ref, o1_ref, o2_ref, o3_ref, l1_ref, l2_ref, l3_ref, yb_ref, ga_ref, gg_ref,
                  ex_ref, wa_ref, wb_ref, wo_ref, out_ref, oslab_ref, lslab_ref, *, dilations):
    tm = x_ref.shape[0]
    slabs = oslab_ref.shape[0]

    def natural_lse(l_ref, d):
        if d == 1:
            return l_ref[...]
        for r in range(d):
            lslab_ref[pl.ds(r, tm // d, stride=d), :] = l_ref[:, pl.ds(r * LANES, LANES)]
        return lslab_ref[...]

    lses = [natural_lse(l, d) for l, d in zip((l1_ref, l2_ref, l3_ref), dilations)]
    top = jnp.maximum(jnp.maximum(lses[0], lses[1]), lses[2])
    ws = [jnp.exp2(l - top) for l in lses]
    inv = 1.0 / (ws[0] + ws[1] + ws[2])
    ya = None
    for w, o_ref, d in zip(ws, (o1_ref, o2_ref, o3_ref), dilations):
        if d == 1:
            o = o_ref[...].astype(F32)
        else:
            for r in range(d):
                for c in range(slabs):
                    oslab_ref[c, pl.ds(r, tm // d, stride=d), :] = (
                        o_ref[:, pl.ds((r * slabs + c) * LANES, LANES)].astype(F32))
            o = jnp.concatenate([oslab_ref[c] for c in range(slabs)], axis=1)
        term = _dot((w * inv).astype(BF16), ex_ref[...]) * o
        ya = term if ya is None else ya + term
    a = _dot(ya.astype(BF16), wa_ref[...])
    b = _dot(yb_ref[...], wb_ref[...])
    merged = ga_ref[...].astype(F32) * a + gg_ref[...].astype(F32) * b
    out_ref[...] = x_ref[...] + _dot(merged.astype(BF16), wo_ref[...])


def _merge_call(x2, os_, lses, yb, ga, gg, wa, wb, wo, *, tm, dilations):
    m, d = x2.shape
    width = yb.shape[1]
    expand = np.zeros((LANES, width), np.float32)
    for h in range(width // A_HEAD_DIM):
        expand[h, h * A_HEAD_DIM:(h + 1) * A_HEAD_DIM] = 1.0
    row = lambda n: pl.BlockSpec((tm, n), lambda i: (i, 0))
    dil = lambda n: [pl.BlockSpec((tm // dl, dl * n), lambda i: (i, 0)) for dl in dilations]
    return pl.pallas_call(
        functools.partial(_merge_kernel, dilations=dilations),
        grid=(m // tm,),
        in_specs=[row(d)] + dil(width) + dil(LANES) + [row(width), row(d), row(d)]
        + [_const_spec((LANES, width)), _const_spec(wa.shape), _const_spec(wb.shape),
           _const_spec(wo.shape)],
        out_specs=row(d),
        out_shape=jax.ShapeDtypeStruct((m, d), F32),
        scratch_shapes=[pltpu.VMEM((width // LANES, tm, LANES), F32), pltpu.VMEM((tm, LANES), F32)],
        compiler_params=_params("parallel"),
        name="merge",
    )(x2, *os_, *lses, yb, ga, gg, jnp.asarray(expand, BF16), wa, wb, wo)


def _ffn_kernel(x_ref, p_ref, nf_ref, np_ref, nl_ref, wg_ref, wu_ref, wd_ref, wpe_ref, wpg_ref,
                out_ref, *, chunk):
    x = x_ref[...]
    h = _rms(x, nf_ref[...]).astype(BF16)
    hidden = wg_ref.shape[1]
    acc = x
    for c in range(0, hidden, chunk):
        g = _dot(h, wg_ref[:, c:c + chunk])
        u = _dot(h, wu_ref[:, c:c + chunk])
        acc = acc + _dot((g * jax.nn.sigmoid(g) * u).astype(BF16), wd_ref[c:c + chunk, :])
    hp = _rms(acc, np_ref[...]).astype(BF16)
    pe = _dot(p_ref[...].astype(BF16), wpe_ref[...])
    x3 = acc + pe * jax.nn.sigmoid(_dot(hp, wpg_ref[...]))
    out_ref[...] = _rms(x3, nl_ref[...])


def _ffn_call(x2, p2, nf, npl, nl, wg, wu, wd, wpe, wpg, *, tm, chunk):
    m, d = x2.shape
    row = lambda n: pl.BlockSpec((tm, n), lambda i: (i, 0))
    return pl.pallas_call(
        functools.partial(_ffn_kernel, chunk=chunk),
        grid=(m // tm,),
        in_specs=[row(d), row(p2.shape[1])] + [_const_spec((1, d))] * 3
        + [_const_spec(w.shape) for w in (wg, wu, wd, wpe, wpg)],
        out_specs=row(d),
        out_shape=jax.ShapeDtypeStruct((m, d), F32),
        compiler_params=_params("parallel"),
        name="ffn",
    )(x2, p2, nf, npl, nl, wg, wu, wd, wpe, wpg)


def kernel(x, p, norm_mix, w_in, hg_lb, hg_norm, w_a_up, w_b_up, w_out, norm_ffn, w_gu, w_down,
           norm_ple, w_pe, w_pg, norm_final):
    batch, seq, d = x.shape
    depth = p.shape[0]
    assert depth == 1 and hg_lb.shape[0] == 2
    m = batch * seq
    tm = 512
    hidden = w_down.shape[1]
    bf = lambda w: w.astype(BF16)
    vec = lambda g: g.reshape(1, -1).astype(F32)

    dilations = tuple(dl for _, dl in A_PATTERNS)
    x2 = x.reshape(m, d)
    outs = _front_call(x2, vec(norm_mix[0]), bf(w_in[0]), hg_lb.astype(F32), vec(hg_norm[0]),
                       batch=batch, seq=seq, tm=tm, dilations=dilations)
    nd = len(dilations)
    qs, ks, vs = outs[:nd], outs[nd:2 * nd], outs[2 * nd:3 * nd]
    yb, ga, gg = outs[3 * nd:]
    os_, lses = [], []
    for g, (window, dilation) in enumerate(A_PATTERNS):
        o, lse = _attn_call(qs[g], ks[g], vs[g], batch=batch, seq=seq, window=window, dilation=dilation)
        os_.append(o)
        lses.append(lse)
    x2 = _merge_call(x2, os_, lses, yb, ga, gg, bf(w_a_up[0]), bf(w_b_up[0]), bf(w_out[0]), tm=tm,
                     dilations=dilations)
    out = _ffn_call(x2, p[0].reshape(m, -1), vec(norm_ffn[0]), vec(norm_ple[0]), vec(norm_final),
                    bf(w_gu[0, :, :hidden]), bf(w_gu[0, :, hidden:]), bf(w_down[0]), bf(w_pe[0]),
                    bf(w_pg[0]), tm=tm, chunk=256)
    return out.reshape(batch, seq, d)
```

```python
import functools

import jax
import jax.numpy as jnp
import numpy as np
from jax import lax
from jax.experimental import pallas as pl
from jax.experimental.pallas import tpu as pltpu

F32 = jnp.float32
BF16 = jnp.bfloat16

LANES = 128
EPS = 1e-6
NEG = -1e30
LOG2E = 1.4426950408889634

A_HEAD_DIM = 64
A_PATTERNS = ((128, 1), (512, 4), (2048, 16))
A_BLK = 128
ATTN_UNITS = 8
B_HEAD_DIM = 128
HG_TILE = 256

VMEM_LIMIT = 56 * 1024 * 1024


def _params(*sem):
    return pltpu.CompilerParams(dimension_semantics=sem, vmem_limit_bytes=VMEM_LIMIT)


def _const_spec(shape):
    nd = len(shape)
    return pl.BlockSpec(shape, lambda *_: (0,) * nd, pipeline_mode=pl.Buffered(1))


def _dot(a, b):
    return jnp.dot(a, b, preferred_element_type=F32)


def _dot_nt(a, b):
    return lax.dot_general(a, b, (((1,), (1,)), ((), ())), preferred_element_type=F32)


def _dot_tn(a, b):
    return lax.dot_general(a, b, (((0,), (0,)), ((), ())), preferred_element_type=F32)


def _rms(x, g):
    return x * lax.rsqrt(jnp.mean(x * x, axis=-1, keepdims=True) + EPS) * g


SUB = 8
HG_BLOCK = 64
HG_SAFE_LOG2 = 100.0


def _rep(t, n):
    return jnp.concatenate([t] * n, axis=0)


def _small_levels(lf2, visit=None):
    sub = lax.broadcasted_iota(jnp.int32, (1, SUB, lf2.shape[-1]), 1)
    pre = tot = lf2.reshape(lf2.shape[0] // SUB, SUB, lf2.shape[-1])
    m, level = 1, 1
    while m < SUB:
        upper = (sub & m) != 0
        if visit is not None:
            visit(level, upper, pre, tot)
        below = pltpu.roll(tot, m, 1)
        above = pltpu.roll(tot, SUB - m, 1)
        pre = pre + jnp.where(upper, below, 0.0)
        tot = tot + jnp.where(upper, below, above)
        m, level = 2 * m, level + 1
    return pre, tot


def _log_decay(lf2):
    sub = lax.broadcasted_iota(jnp.int32, (1, SUB, lf2.shape[-1]), 1)
    pre = lf2.reshape(lf2.shape[0] // SUB, SUB, lf2.shape[-1])
    k = 1
    while k < SUB:
        pre = pre + jnp.where(sub >= k, pltpu.roll(pre, k, 1), 0.0)
        k *= 2
    tot = jnp.broadcast_to(pre[:, SUB - 1:SUB, :], pre.shape)
    ends = [tot[0]]
    for g in range(1, pre.shape[0]):
        ends.append(ends[-1] + tot[g])
    b = jnp.concatenate([pre[0]] + [pre[g] + ends[g - 1] for g in range(1, len(ends))], axis=0)
    return b, ends


def _hgrn_unit(q, kk, lf2, b, ends, v, gact, gn, lvl, state_ref, head, block_start):
    tile = q.shape[0]
    groups, half = tile // SUB, tile // 2
    quads = [slice(r0, r0 + half) for r0 in (0, half)]

    def level_factors(m):
        parts = []
        for lo in range(0, tile, 2 * m):
            mid = _rep(ends[(lo + m) // SUB - 1], m // SUB)
            parts.append(kk[lo:lo + m] * jnp.exp2(mid - b[lo:lo + m]))
            parts.append(q[lo + m:lo + 2 * m] * jnp.exp2(b[lo + m:lo + 2 * m] - mid))
        return jnp.concatenate(parts, axis=0)

    def add_level(level, y):
        yb16 = y.astype(BF16)
        for i, rows in enumerate(quads):
            diag[i] = jnp.where(lvl == level, _dot_nt(yb16[rows], yb16[rows]), diag[i])

    if block_start:
        qs, ks = [], []
        for lo in range(0, tile, HG_BLOCK):
            d = b[lo:lo + HG_BLOCK]
            if lo:
                d = d - _rep(ends[lo // SUB - 1], HG_BLOCK // SUB)
            qs.append(q[lo:lo + HG_BLOCK] * jnp.exp2(d))
            ks.append(kk[lo:lo + HG_BLOCK] * jnp.exp2(-d))
        qf = jnp.concatenate(qs, axis=0).astype(BF16)
        kf = jnp.concatenate(ks, axis=0).astype(BF16)
        diag = [jnp.where(lvl >= 0, _dot_nt(qf[rows], kf[rows]), 0.0) for rows in quads]
        m, level = HG_BLOCK, HG_BLOCK.bit_length()
    else:
        qb16, kb16 = q.astype(BF16), kk.astype(BF16)
        diag = [jnp.where(lvl == 0, _dot_nt(qb16[rows], kb16[rows]), 0.0) for rows in quads]
        g3 = lambda t: t.reshape(groups, SUB, t.shape[-1])
        q3, k3 = g3(q), g3(kk)

        def small(level, upper, pre, tot):
            y3 = jnp.where(upper, q3, k3) * jnp.exp2(jnp.where(upper, pre, tot - pre))
            add_level(level, y3.reshape(tile, -1))

        _small_levels(lf2, small)
        m, level = SUB, SUB.bit_length()
    while m < half:
        add_level(level, level_factors(m))
        m, level = 2 * m, level + 1
    top = level_factors(half).astype(BF16)
    cross = _dot_nt(top[half:], top[:half]).astype(BF16)
    diag = [sc.astype(BF16) for sc in diag]

    st = state_ref[head]
    whole = _rep(ends[-1], groups)
    inter = _dot_nt((q * jnp.exp2(b)).astype(BF16), st.astype(BF16))
    o = inter + jnp.concatenate(
        [_dot(diag[0], v[:half]), _dot(jnp.concatenate([cross, diag[1]], axis=1), v)], axis=0)
    k_end = (kk * jnp.exp2(whole - b)).astype(BF16)
    state_ref[head] = st * jnp.exp2(ends[-1][0:1, :]) + _dot_tn(v, k_end)
    return _rms(o, gn) * gact


def _front_kernel(lvl_ref, hglb_ref, gn_ref, x_ref, g_ref, w_ref, *refs, width, dilations):
    nd = len(dilations)
    attn_refs = refs[:3 * nd]
    yb_ref, ga_ref, gg_ref, hx_ref, slab_ref, state_ref = refs[3 * nd:]
    tm = x_ref.shape[0]
    heads = width // B_HEAD_DIM

    @pl.when(pl.program_id(1) == 0)
    def _():
        state_ref[...] = jnp.zeros_like(state_ref)

    hx_ref[...] = _rms(x_ref[...], g_ref[...]).astype(BF16)
    half = tm // 2
    slabs = width // LANES

    def proj(t0, rows, c0, n):
        return _dot(hx_ref[pl.ds(t0, rows), :], w_ref[:, c0:c0 + n])

    def attn_job(a, t0):
        y = proj(t0, half, a * width, width)
        if a == 0:
            y = y * (A_HEAD_DIM ** -0.5 * LOG2E)
        for c in range(slabs):
            slab_ref[0, a * slabs + c, pl.ds(t0, half), :] = y[:, c * LANES:(c + 1) * LANES]
        d_prev = 1
        for k, (out_ref, d) in enumerate(zip(attn_refs[a * nd:(a + 1) * nd], dilations)):
            if d == 1:
                out_ref[pl.ds(t0, half), :] = y.astype(BF16)
                continue
            ratio, n_prev, n = d // d_prev, half // d_prev, half // d
            for r in range(d):
                r_prev, j = r % d_prev, r // d_prev
                for c in range(slabs):
                    piece = slab_ref[k - 1, a * slabs + c,
                                     pl.ds(t0 + r_prev * n_prev + j, n, stride=ratio), :]
                    out_ref[pl.ds(t0 // d, n), pl.ds((r * slabs + c) * LANES, LANES)] = piece.astype(BF16)
                    if k + 1 < nd:
                        slab_ref[k, a * slabs + c, pl.ds(t0 + r * n, n), :] = piece
            d_prev = d

    def gate_job(out_ref, c0, j, t0):
        y = proj(t0, half, c0 + j * width, width)
        out_ref[pl.ds(t0, half), pl.ds(j * width, width)] = jax.nn.sigmoid(y).astype(BF16)

    jobs = [functools.partial(attn_job, a, t0) for a in range(3) for t0 in (0, half)]
    jobs += [functools.partial(gate_job, ref, c0, j, t0)
             for ref, c0 in ((ga_ref, 7 * width), (gg_ref, 9 * width)) for j in range(2)
             for t0 in (0, half)]

    hg = hglb_ref[...]
    ex = jnp.exp(hg - jnp.max(hg, axis=0, keepdims=True))
    lb_all = ex[0:1] / jnp.sum(ex, axis=0, keepdims=True)
    lvl = lvl_ref[...]

    def head_proj(hd):
        hp = proj(0, tm, 3 * width + 4 * hd * B_HEAD_DIM, 4 * B_HEAD_DIM)
        q, fgate, v, gate = (hp[:, j * B_HEAD_DIM:(j + 1) * B_HEAD_DIM] for j in range(4))
        lb = lb_all[:, hd * B_HEAD_DIM:(hd + 1) * B_HEAD_DIM]
        f = lb + (1.0 - lb) * jax.nn.sigmoid(fgate)
        return q, 1.0 - f, jnp.log2(f), v.astype(BF16), gate * jax.nn.sigmoid(gate)

    units, worst = [], None
    for hd in range(heads):
        data = head_proj(hd)
        for t0 in range(0, tm, HG_TILE):
            q, kk, lf2, v, gact = (t[t0:t0 + HG_TILE] for t in data)
            b, ends = _log_decay(lf2)
            for lo in range(0, HG_TILE, HG_BLOCK):
                blk = ends[(lo + HG_BLOCK) // SUB - 1]
                if lo:
                    blk = blk - ends[lo // SUB - 1]
                worst = blk if worst is None else jnp.minimum(worst, blk)
            units.append((hd, t0, q, kk, lf2, b, ends, v, gact))
    for job in jobs:
        job()

    def run(block_start):
        for hd, t0, *args in units:
            y = _hgrn_unit(*args, gn_ref[...], lvl, state_ref, hd, block_start)
            yb_ref[pl.ds(t0, HG_TILE), pl.ds(hd * B_HEAD_DIM, B_HEAD_DIM)] = y.astype(yb_ref.dtype)

    safe = jnp.min(worst) >= -HG_SAFE_LOG2
    pl.when(safe)(functools.partial(run, True))
    pl.when(jnp.logical_not(safe))(functools.partial(run, False))


def _level_map(tile):
    t = np.arange(tile)[:, None]
    s = np.arange(tile)[None, :]
    x = np.maximum(t ^ s, 1)
    lv = np.floor(np.log2(x)).astype(np.int32) + 1
    return jnp.asarray(np.where(s < t, lv, np.where(s == t, 0, -1)), jnp.int32)


def _front_call(x2, g, w, hg_lb, g_norm, *, batch, seq, tm, dilations):
    m, d = x2.shape
    width = d // 2
    assert seq % tm == 0 and tm % HG_TILE == 0
    assert dilations[0] == 1 and all(b % a == 0 for a, b in zip(dilations, dilations[1:]))
    steps = seq // tm
    row = lambda n: pl.BlockSpec((tm, n), lambda b, t: (b * steps + t, 0))
    sds = lambda n, dt: jax.ShapeDtypeStruct((m, n), dt)
    dil_specs = [pl.BlockSpec((tm // dl, dl * width), lambda b, t: (b * steps + t, 0))
                 for dl in dilations] * 3
    dil_shapes = [jax.ShapeDtypeStruct((m // dl, dl * width), BF16) for dl in dilations] * 3
    return pl.pallas_call(
        functools.partial(_front_kernel, width=width, dilations=dilations),
        grid=(batch, steps),
        in_specs=[_const_spec((HG_TILE // 2, HG_TILE // 2)), _const_spec(hg_lb.shape),
                  _const_spec((1, B_HEAD_DIM)), row(d), _const_spec((1, d)), _const_spec(w.shape)],
        out_specs=dil_specs + [row(width), row(d), row(d)],
        out_shape=dil_shapes + [sds(width, BF16), sds(d, BF16), sds(d, BF16)],
        scratch_shapes=[pltpu.VMEM((tm, d), BF16),
                        pltpu.VMEM((len(dilations) - 1, 3 * width // LANES, tm, LANES), F32),
                        pltpu.VMEM((width // B_HEAD_DIM, B_HEAD_DIM, B_HEAD_DIM), F32)],
        compiler_params=_params("parallel", "arbitrary"),
        name="front",
    )(_level_map(HG_TILE // 2), hg_lb, g_norm, x2, g, _head_major(w, width))


def _head_major(w, width):
    heads = width // B_HEAD_DIM
    hg = w[:, 3 * width:7 * width].reshape(w.shape[0], 4, heads, B_HEAD_DIM)
    hg = hg.transpose(0, 2, 1, 3).reshape(w.shape[0], 4 * width)
    return jnp.concatenate([w[:, :3 * width], hg, w[:, 7 * width:]], axis=1)


def _attn_kernel(bias_ref, q_ref, k_ref, v_ref, o_ref, lse_ref, *, tq, rblk):
    width = q_ref.shape[2] // rblk
    pairs = width // LANES
    lane = lax.broadcasted_iota(jnp.int32, (A_BLK, LANES), 1)
    low_half = lane < A_HEAD_DIM
    ones = jnp.ones((2 * A_BLK, LANES), BF16)
    for rr in range(rblk):
        for qi in range(tq // A_BLK):
            n = pl.program_id(2) * (tq // A_BLK) + qi
            start = pl.multiple_of(jnp.maximum(n - 1, 0) * A_BLK, A_BLK)
            bias = bias_ref[jnp.where(n == 0, 1, 0)]
            rows = pl.ds(qi * A_BLK, A_BLK)
            mx_tile = jnp.zeros((A_BLK, LANES), F32)
            den_tile = jnp.ones((A_BLK, LANES), F32)
            for p in range(pairs):
                cols = pl.ds(rr * width + p * LANES, LANES)
                q_pair = q_ref[0, rows, cols]
                k_band = k_ref[0, pl.ds(start, 2 * A_BLK), cols]
                v_cat = jnp.concatenate([v_ref[0, pl.ds(start, 2 * A_BLK), cols], ones], axis=1)
                nums, dens = [], []
                for hh in range(2):
                    mine = low_half if hh == 0 else jnp.logical_not(low_half)
                    s = _dot_nt(jnp.where(mine, q_pair, jnp.zeros_like(q_pair)), k_band) + bias
                    mx = jnp.max(s, axis=-1, keepdims=True)
                    r = _dot(jnp.exp2(s - mx).astype(BF16), v_cat)
                    nums.append(r[:, :LANES])
                    dens.append(r[:, LANES:])
                    mx_tile = jnp.where(lane == 2 * p + hh, mx, mx_tile)
                    den_tile = jnp.where(lane == 2 * p + hh, dens[-1], den_tile)
                o_ref[0, rows, cols] = (jnp.where(low_half, nums[0], nums[1])
                                        / jnp.where(low_half, dens[0], dens[1])).astype(o_ref.dtype)
            lse_ref[0, rows, pl.ds(rr * LANES, LANES)] = mx_tile + jnp.log2(den_tile)


def _band_bias(span):
    a = np.arange(A_BLK)[:, None]
    c = np.arange(2 * A_BLK)[None, :]
    rel = a + A_BLK - c
    normal = (rel >= 0) & (rel <= span)
    rel0 = a - c
    first = (rel0 >= 0) & (rel0 <= span)
    return jnp.asarray(np.where(np.stack([normal, first]), 0.0, NEG), F32)


def _attn_call(q, k, v, *, batch, seq, window, dilation):
    width = q.shape[1] // dilation
    assert seq % (dilation * A_BLK) == 0 and window % dilation == 0
    length = seq // dilation
    tq = min(length, ATTN_UNITS * A_BLK)
    rblk = min(dilation, ATTN_UNITS * A_BLK // tq)
    sub = lambda t: t.reshape(batch, length, dilation * width)
    o, lse = pl.pallas_call(
        functools.partial(_attn_kernel, tq=tq, rblk=rblk),
        grid=(batch, dilation // rblk, length // tq),
        in_specs=[
            _const_spec((2, A_BLK, 2 * A_BLK)),
            pl.BlockSpec((1, tq, rblk * width), lambda b, r, t: (b, t, r)),
            pl.BlockSpec((1, length, rblk * width), lambda b, r, t: (b, 0, r)),
            pl.BlockSpec((1, length, rblk * width), lambda b, r, t: (b, 0, r)),
        ],
        out_specs=[
            pl.BlockSpec((1, tq, rblk * width), lambda b, r, t: (b, t, r)),
            pl.BlockSpec((1, tq, rblk * LANES), lambda b, r, t: (b, t, r)),
        ],
        out_shape=[
            jax.ShapeDtypeStruct((batch, length, dilation * width), BF16),
            jax.ShapeDtypeStruct((batch, length, dilation * LANES), F32),
        ],
        compiler_params=_params("parallel", "parallel", "arbitrary"),
        name=f"attn_d{dilation}",
    )(_band_bias(window // dilation), sub(q), sub(k), sub(v))
    return (o.reshape(batch * length, dilation * width), lse.reshape(batch * length, dilation * LANES))


def _merge_kernel(x_ref, o1_ref, o2_ref, o3_ref, l1_ref, l2_ref, l3_ref, yb_ref, ga_ref, gg_ref,
                  ex_ref, wa_ref, wb_ref, wo_ref, out_ref, oslab_ref, lslab_ref, *, dilations):
    tm = x_ref.shape[0]
    slabs = oslab_ref.shape[0]
    b = _dot(yb_ref[...], wb_ref[...])

    def natural_lse(l_ref, d):
        if d == 1:
            return l_ref[...]
        for r in range(d):
            lslab_ref[pl.ds(r, tm // d, stride=d), :] = l_ref[:, pl.ds(r * LANES, LANES)]
        return lslab_ref[...]

    lses = [natural_lse(l, d) for l, d in zip((l1_ref, l2_ref, l3_ref), dilations)]
    top = jnp.maximum(jnp.maximum(lses[0], lses[1]), lses[2])
    ws = [jnp.exp2(l - top) for l in lses]
    inv = 1.0 / (ws[0] + ws[1] + ws[2])
    ya = None
    for w, o_ref, d in zip(ws, (o1_ref, o2_ref, o3_ref), dilations):
        if d == 1:
            o = o_ref[...].astype(F32)
        else:
            for r in range(d):
                for c in range(slabs):
                    oslab_ref[c, pl.ds(r, tm // d, stride=d), :] = (
                        o_ref[:, pl.ds((r * slabs + c) * LANES, LANES)].astype(F32))
            o = jnp.concatenate([oslab_ref[c] for c in range(slabs)], axis=1)
        term = _dot((w * inv).astype(BF16), ex_ref[...]) * o
        ya = term if ya is None else ya + term
    a = _dot(ya.astype(BF16), wa_ref[...])
    merged = ga_ref[...].astype(F32) * a + gg_ref[...].astype(F32) * b
    out_ref[...] = x_ref[...] + _dot(merged.astype(BF16), wo_ref[...])


def _merge_call(x2, os_, lses, yb, ga, gg, wa, wb, wo, *, tm, dilations):
    m, d = x2.shape
    width = yb.shape[1]
    expand = np.zeros((LANES, width), np.float32)
    for h in range(width // A_HEAD_DIM):
        expand[h, h * A_HEAD_DIM:(h + 1) * A_HEAD_DIM] = 1.0
    row = lambda n: pl.BlockSpec((tm, n), lambda i: (i, 0))
    dil = lambda n: [pl.BlockSpec((tm // dl, dl * n), lambda i: (i, 0)) for dl in dilations]
    return pl.pallas_call(
        functools.partial(_merge_kernel, dilations=dilations),
        grid=(m // tm,),
        in_specs=[row(d)] + dil(width) + dil(LANES) + [row(width), row(d), row(d)]
        + [_const_spec((LANES, width)), _const_spec(wa.shape), _const_spec(wb.shape),
           _const_spec(wo.shape)],
        out_specs=row(d),
        out_shape=jax.ShapeDtypeStruct((m, d), F32),
        scratch_shapes=[pltpu.VMEM((width // LANES, tm, LANES), F32), pltpu.VMEM((tm, LANES), F32)],
        compiler_params=_params("parallel"),
        name="merge",
    )(x2, *os_, *lses, yb, ga, gg, jnp.asarray(expand, BF16), wa, wb, wo)


def _ffn_kernel(x_ref, p_ref, nf_ref, np_ref, nl_ref, wgu_ref, wd_ref, wpe_ref, wpg_ref,
                out_ref, *, chunk):
    x = x_ref[...]
    h = _rms(x, nf_ref[...]).astype(BF16)
    hidden = wd_ref.shape[0]
    acc = x
    for c in range(0, hidden, chunk):
        gu = _dot(h, wgu_ref[:, 2 * c:2 * (c + chunk)])
        g, u = gu[:, :chunk], gu[:, chunk:]
        acc = acc + _dot((g * jax.nn.sigmoid(g) * u).astype(BF16), wd_ref[c:c + chunk, :])
    hp = _rms(acc, np_ref[...]).astype(BF16)
    pe = _dot(p_ref[...].astype(BF16), wpe_ref[...])
    x3 = acc + pe * jax.nn.sigmoid(_dot(hp, wpg_ref[...]))
    out_ref[...] = _rms(x3, nl_ref[...])


def _ffn_call(x2, p2, nf, npl, nl, w_gu, wd, wpe, wpg, *, tm, chunk):
    m, d = x2.shape
    hidden = wd.shape[0]
    assert hidden % chunk == 0
    wgu = w_gu.reshape(d, 2, hidden // chunk, chunk).transpose(0, 2, 1, 3).reshape(d, 2 * hidden)
    row = lambda n: pl.BlockSpec((tm, n), lambda i: (i, 0))
    return pl.pallas_call(
        functools.partial(_ffn_kernel, chunk=chunk),
        grid=(m // tm,),
        in_specs=[row(d), row(p2.shape[1])] + [_const_spec((1, d))] * 3
        + [_const_spec(w.shape) for w in (wgu, wd, wpe, wpg)],
        out_specs=row(d),
        out_shape=jax.ShapeDtypeStruct((m, d), F32),
        compiler_params=_params("parallel"),
        name="ffn",
    )(x2, p2, nf, npl, nl, wgu, wd, wpe, wpg)


def kernel(x, p, norm_mix, w_in, hg_lb, hg_norm, w_a_up, w_b_up, w_out, norm_ffn, w_gu, w_down,
           norm_ple, w_pe, w_pg, norm_final):
    batch, seq, d = x.shape
    depth = p.shape[0]
    assert depth == 1 and hg_lb.shape[0] == 2
    m = batch * seq
    tm = 512
    hidden = w_down.shape[1]
    bf = lambda w: w.astype(BF16)
    vec = lambda g: g.reshape(1, -1).astype(F32)

    dilations = tuple(dl for _, dl in A_PATTERNS)
    x2 = x.reshape(m, d)
    outs = _front_call(x2, vec(norm_mix[0]), bf(w_in[0]), hg_lb.astype(F32), vec(hg_norm[0]),
                       batch=batch, seq=seq, tm=tm, dilations=dilations)
    nd = len(dilations)
    qs, ks, vs = outs[:nd], outs[nd:2 * nd], outs[2 * nd:3 * nd]
    yb, ga, gg = outs[3 * nd:]
    os_, lses = [], []
    for g, (window, dilation) in enumerate(A_PATTERNS):
        o, lse = _attn_call(qs[g], ks[g], vs[g], batch=batch, seq=seq, window=window, dilation=dilation)
        os_.append(o)
        lses.append(lse)
    x2 = _merge_call(x2, os_, lses, yb, ga, gg, bf(w_a_up[0]), bf(w_b_up[0]), bf(w_out[0]), tm=tm,
                     dilations=dilations)
    out = _ffn_call(x2, p[0].reshape(m, -1), vec(norm_ffn[0]), vec(norm_ple[0]), vec(norm_final),
                    bf(w_gu[0]), bf(w_down[0]), bf(w_pe[0]), bf(w_pg[0]), tm=tm, chunk=256)
    return out.reshape(batch, seq, d)
```

```python
import functools

import jax
import jax.numpy as jnp
import numpy as np
from jax import lax
from jax.experimental import pallas as pl
from jax.experimental.pallas import tpu as pltpu

F32 = jnp.float32
BF16 = jnp.bfloat16

LANES = 128
EPS = 1e-6
NEG = -1e30
LOG2E = 1.4426950408889634

A_HEAD_DIM = 64
A_PATTERNS = ((128, 1), (512, 4), (2048, 16))
A_BLK = 128
ATTN_UNITS = 8
B_HEAD_DIM = 128
HG_TILE = 256

VMEM_LIMIT = 56 * 1024 * 1024


def _params(*sem):
    return pltpu.CompilerParams(dimension_semantics=sem, vmem_limit_bytes=VMEM_LIMIT)


def _const_spec(shape):
    nd = len(shape)
    return pl.BlockSpec(shape, lambda *_: (0,) * nd, pipeline_mode=pl.Buffered(1))


def _dot(a, b):
    return jnp.dot(a, b, preferred_element_type=F32)


def _dot_nt(a, b):
    return lax.dot_general(a, b, (((1,), (1,)), ((), ())), preferred_element_type=F32)


def _dot_tn(a, b):
    return lax.dot_general(a, b, (((0,), (0,)), ((), ())), preferred_element_type=F32)


def _rms(x, g):
    return x * lax.rsqrt(jnp.mean(x * x, axis=-1, keepdims=True) + EPS) * g


SUB = 8
HG_BLOCK = 64
HG_SAFE_LOG2 = 100.0


def _rep(t, n):
    return jnp.concatenate([t] * n, axis=0)


def _small_levels(lf2, visit=None):
    sub = lax.broadcasted_iota(jnp.int32, (1, SUB, lf2.shape[-1]), 1)
    pre = tot = lf2.reshape(lf2.shape[0] // SUB, SUB, lf2.shape[-1])
    m, level = 1, 1
    while m < SUB:
        upper = (sub & m) != 0
        if visit is not None:
            visit(level, upper, pre, tot)
        below = pltpu.roll(tot, m, 1)
        above = pltpu.roll(tot, SUB - m, 1)
        pre = pre + jnp.where(upper, below, 0.0)
        tot = tot + jnp.where(upper, below, above)
        m, level = 2 * m, level + 1
    return pre, tot


def _log_decay(lf2):
    sub = lax.broadcasted_iota(jnp.int32, (1, SUB, lf2.shape[-1]), 1)
    pre = lf2.reshape(lf2.shape[0] // SUB, SUB, lf2.shape[-1])
    k = 1
    while k < SUB:
        pre = pre + jnp.where(sub >= k, pltpu.roll(pre, k, 1), 0.0)
        k *= 2
    tot = jnp.broadcast_to(pre[:, SUB - 1:SUB, :], pre.shape)
    ends = [tot[0]]
    for g in range(1, pre.shape[0]):
        ends.append(ends[-1] + tot[g])
    b = jnp.concatenate([pre[0]] + [pre[g] + ends[g - 1] for g in range(1, len(ends))], axis=0)
    return b, ends


def _hgrn_unit(q, kk, lf2, b, ends, v, gact, gn, lvl, state_ref, head, block_start):
    tile = q.shape[0]
    groups, half = tile // SUB, tile // 2
    quads = [slice(r0, r0 + half) for r0 in (0, half)]

    def level_factors(m):
        parts = []
        for lo in range(0, tile, 2 * m):
            mid = _rep(ends[(lo + m) // SUB - 1], m // SUB)
            parts.append(kk[lo:lo + m] * jnp.exp2(mid - b[lo:lo + m]))
            parts.append(q[lo + m:lo + 2 * m] * jnp.exp2(b[lo + m:lo + 2 * m] - mid))
        return jnp.concatenate(parts, axis=0)

    def add_level(level, y):
        yb16 = y.astype(BF16)
        for i, rows in enumerate(quads):
            diag[i] = jnp.where(lvl == level, _dot_nt(yb16[rows], yb16[rows]), diag[i])

    if block_start:
        qs, ks = [], []
        for lo in range(0, tile, HG_BLOCK):
            d = b[lo:lo + HG_BLOCK]
            if lo:
                d = d - _rep(ends[lo // SUB - 1], HG_BLOCK // SUB)
            qs.append(q[lo:lo + HG_BLOCK] * jnp.exp2(d))
            ks.append(kk[lo:lo + HG_BLOCK] * jnp.exp2(-d))
        qf = jnp.concatenate(qs, axis=0).astype(BF16)
        kf = jnp.concatenate(ks, axis=0).astype(BF16)
        diag = [jnp.where(lvl >= 0, _dot_nt(qf[rows], kf[rows]), 0.0) for rows in quads]
        m, level = HG_BLOCK, HG_BLOCK.bit_length()
    else:
        qb16, kb16 = q.astype(BF16), kk.astype(BF16)
        diag = [jnp.where(lvl == 0, _dot_nt(qb16[rows], kb16[rows]), 0.0) for rows in quads]
        g3 = lambda t: t.reshape(groups, SUB, t.shape[-1])
        q3, k3 = g3(q), g3(kk)

        def small(level, upper, pre, tot):
            y3 = jnp.where(upper, q3, k3) * jnp.exp2(jnp.where(upper, pre, tot - pre))
            add_level(level, y3.reshape(tile, -1))

        _small_levels(lf2, small)
        m, level = SUB, SUB.bit_length()
    while m < half:
        add_level(level, level_factors(m))
        m, level = 2 * m, level + 1
    top = level_factors(half).astype(BF16)
    cross = _dot_nt(top[half:], top[:half]).astype(BF16)
    diag = [sc.astype(BF16) for sc in diag]

    st = state_ref[head]
    whole = _rep(ends[-1], groups)
    inter = _dot_nt((q * jnp.exp2(b)).astype(BF16), st.astype(BF16))
    o = inter + jnp.concatenate(
        [_dot(diag[0], v[:half]), _dot(jnp.concatenate([cross, diag[1]], axis=1), v)], axis=0)
    k_end = (kk * jnp.exp2(whole - b)).astype(BF16)
    state_ref[head] = st * jnp.exp2(ends[-1][0:1, :]) + _dot_tn(v, k_end)
    return _rms(o, gn) * gact


def _front_kernel(lvl_ref, hglb_ref, gn_ref, x_ref, g_ref, w_ref, *refs, width, dilations):
    nd = len(dilations)
    attn_refs = refs[:3 * nd]
    yb_ref, ga_ref, gg_ref, hx_ref, slab_ref, state_ref = refs[3 * nd:]
    tm = x_ref.shape[0]
    heads = width // B_HEAD_DIM

    @pl.when(pl.program_id(1) == 0)
    def _():
        state_ref[...] = jnp.zeros_like(state_ref)

    hx_ref[...] = _rms(x_ref[...], g_ref[...]).astype(BF16)
    half = tm // 2
    slabs = width // LANES

    def proj(t0, rows, c0, n):
        return _dot(hx_ref[pl.ds(t0, rows), :], w_ref[:, c0:c0 + n])

    def attn_job(a, t0):
        y = proj(t0, half, a * width, width)
        if a == 0:
            y = y * (A_HEAD_DIM ** -0.5 * LOG2E)
        for c in range(slabs):
            slab_ref[0, a * slabs + c, pl.ds(t0, half), :] = y[:, c * LANES:(c + 1) * LANES]
        d_prev = 1
        for k, (out_ref, d) in enumerate(zip(attn_refs[a * nd:(a + 1) * nd], dilations)):
            if d == 1:
                out_ref[pl.ds(t0, half), :] = y.astype(BF16)
                continue
            ratio, n_prev, n = d // d_prev, half // d_prev, half // d
            for r in range(d):
                r_prev, j = r % d_prev, r // d_prev
                for c in range(slabs):
                    piece = slab_ref[k - 1, a * slabs + c,
                                     pl.ds(t0 + r_prev * n_prev + j, n, stride=ratio), :]
                    out_ref[pl.ds(t0 // d, n), pl.ds((r * slabs + c) * LANES, LANES)] = piece.astype(BF16)
                    if k + 1 < nd:
                        slab_ref[k, a * slabs + c, pl.ds(t0 + r * n, n), :] = piece
            d_prev = d

    def gate_job(out_ref, c0, j, t0):
        y = proj(t0, half, c0 + j * width, width)
        out_ref[pl.ds(t0, half), pl.ds(j * width, width)] = jax.nn.sigmoid(y).astype(BF16)

    jobs = [functools.partial(attn_job, a, t0) for a in range(3) for t0 in (0, half)]
    jobs += [functools.partial(gate_job, ref, c0, j, t0)
             for ref, c0 in ((ga_ref, 7 * width), (gg_ref, 9 * width)) for j in range(2)
             for t0 in (0, half)]

    hg = hglb_ref[...]
    ex = jnp.exp(hg - jnp.max(hg, axis=0, keepdims=True))
    lb_all = ex[0:1] / jnp.sum(ex, axis=0, keepdims=True)
    lvl = lvl_ref[...]

    def head_proj(hd):
        hp = proj(0, tm, 3 * width + 4 * hd * B_HEAD_DIM, 4 * B_HEAD_DIM)
        q, fgate, v, gate = (hp[:, j * B_HEAD_DIM:(j + 1) * B_HEAD_DIM] for j in range(4))
        lb = lb_all[:, hd * B_HEAD_DIM:(hd + 1) * B_HEAD_DIM]
        f = lb + (1.0 - lb) * jax.nn.sigmoid(fgate)
        return q, 1.0 - f, jnp.log2(f), v.astype(BF16), gate * jax.nn.sigmoid(gate)

    units, worst = [], None
    for hd in range(heads):
        data = head_proj(hd)
        for t0 in range(0, tm, HG_TILE):
            q, kk, lf2, v, gact = (t[t0:t0 + HG_TILE] for t in data)
            b, ends = _log_decay(lf2)
            for lo in range(0, HG_TILE, HG_BLOCK):
                blk = ends[(lo + HG_BLOCK) // SUB - 1]
                if lo:
                    blk = blk - ends[lo // SUB - 1]
                worst = blk if worst is None else jnp.minimum(worst, blk)
            units.append((hd, t0, q, kk, lf2, b, ends, v, gact))
    for job in jobs:
        job()

    def run(block_start):
        for hd, t0, *args in units:
            y = _hgrn_unit(*args, gn_ref[...], lvl, state_ref, hd, block_start)
            yb_ref[pl.ds(t0, HG_TILE), pl.ds(hd * B_HEAD_DIM, B_HEAD_DIM)] = y.astype(yb_ref.dtype)

    safe = jnp.min(worst) >= -HG_SAFE_LOG2
    pl.when(safe)(functools.partial(run, True))
    pl.when(jnp.logical_not(safe))(functools.partial(run, False))


def _level_map(tile):
    t = np.arange(tile)[:, None]
    s = np.arange(tile)[None, :]
    x = np.maximum(t ^ s, 1)
    lv = np.floor(np.log2(x)).astype(np.int32) + 1
    return jnp.asarray(np.where(s < t, lv, np.where(s == t, 0, -1)), jnp.int32)


def _front_call(x2, g, w, hg_lb, g_norm, *, batch, seq, tm, dilations):
    m, d = x2.shape
    width = d // 2
    assert seq % tm == 0 and tm % HG_TILE == 0
    assert dilations[0] == 1 and all(b % a == 0 for a, b in zip(dilations, dilations[1:]))
    steps = seq // tm
    row = lambda n: pl.BlockSpec((tm, n), lambda b, t: (b * steps + t, 0))
    sds = lambda n, dt: jax.ShapeDtypeStruct((m, n), dt)
    dil_specs = [pl.BlockSpec((tm // dl, dl * width), lambda b, t: (b * steps + t, 0))
                 for dl in dilations] * 3
    dil_shapes = [jax.ShapeDtypeStruct((m // dl, dl * width), BF16) for dl in dilations] * 3
    return pl.pallas_call(
        functools.partial(_front_kernel, width=width, dilations=dilations),
        grid=(batch, steps),
        in_specs=[_const_spec((HG_TILE // 2, HG_TILE // 2)), _const_spec(hg_lb.shape),
                  _const_spec((1, B_HEAD_DIM)), row(d), _const_spec((1, d)), _const_spec(w.shape)],
        out_specs=dil_specs + [row(width), row(d), row(d)],
        out_shape=dil_shapes + [sds(width, BF16), sds(d, BF16), sds(d, BF16)],
        scratch_shapes=[pltpu.VMEM((tm, d), BF16),
                        pltpu.VMEM((len(dilations) - 1, 3 * width // LANES, tm, LANES), F32),
                        pltpu.VMEM((width // B_HEAD_DIM, B_HEAD_DIM, B_HEAD_DIM), F32)],
        compiler_params=_params("parallel", "arbitrary"),
        name="front",
    )(_level_map(HG_TILE // 2), hg_lb, g_norm, x2, g, _head_major(w, width))


def _head_major(w, width):
    heads = width // B_HEAD_DIM
    hg = w[:, 3 * width:7 * width].reshape(w.shape[0], 4, heads, B_HEAD_DIM)
    hg = hg.transpose(0, 2, 1, 3).reshape(w.shape[0], 4 * width)
    return jnp.concatenate([w[:, :3 * width], hg, w[:, 7 * width:]], axis=1)


def _attn_kernel(bias_ref, q_ref, k_ref, v_ref, o_ref, lse_ref, *, tq, rblk):
    width = q_ref.shape[2] // rblk
    pairs = width // LANES
    lane = lax.broadcasted_iota(jnp.int32, (A_BLK, LANES), 1)
    low_half = lane < A_HEAD_DIM
    ones = jnp.ones((2 * A_BLK, LANES), BF16)
    for rr in range(rblk):
        for qi in range(tq // A_BLK):
            n = pl.program_id(2) * (tq // A_BLK) + qi
            start = pl.multiple_of(jnp.maximum(n - 1, 0) * A_BLK, A_BLK)
            bias = bias_ref[jnp.where(n == 0, 1, 0)]
            rows = pl.ds(qi * A_BLK, A_BLK)
            mx_tile = jnp.zeros((A_BLK, LANES), F32)
            den_tile = jnp.ones((A_BLK, LANES), F32)
            for p in range(pairs):
                cols = pl.ds(rr * width + p * LANES, LANES)
                q_pair = q_ref[0, rows, cols]
                k_band = k_ref[0, pl.ds(start, 2 * A_BLK), cols]
                v_cat = jnp.concatenate([v_ref[0, pl.ds(start, 2 * A_BLK), cols], ones], axis=1)
                nums, dens = [], []
                for hh in range(2):
                    mine = low_half if hh == 0 else jnp.logical_not(low_half)
                    s = _dot_nt(jnp.where(mine, q_pair, jnp.zeros_like(q_pair)), k_band) + bias
                    mx = jnp.max(s, axis=-1, keepdims=True)
                    r = _dot(jnp.exp2(s - mx).astype(BF16), v_cat)
                    nums.append(r[:, :LANES])
                    dens.append(r[:, LANES:])
                    mx_tile = jnp.where(lane == 2 * p + hh, mx, mx_tile)
                    den_tile = jnp.where(lane == 2 * p + hh, dens[-1], den_tile)
                o_ref[0, rows, cols] = (jnp.where(low_half, nums[0], nums[1])
                                        / jnp.where(low_half, dens[0], dens[1])).astype(o_ref.dtype)
            lse_ref[0, rows, pl.ds(rr * LANES, LANES)] = mx_tile + jnp.log2(den_tile)


def _band_bias(span):
    a = np.arange(A_BLK)[:, None]
    c = np.arange(2 * A_BLK)[None, :]
    rel = a + A_BLK - c
    normal = (rel >= 0) & (rel <= span)
    rel0 = a - c
    first = (rel0 >= 0) & (rel0 <= span)
    return jnp.asarray(np.where(np.stack([normal, first]), 0.0, NEG), F32)


def _attn_call(q, k, v, *, batch, seq, window, dilation):
    width = q.shape[1] // dilation
    assert seq % (dilation * A_BLK) == 0 and window % dilation == 0
    length = seq // dilation
    tq = min(length, ATTN_UNITS * A_BLK)
    rblk = min(dilation, ATTN_UNITS * A_BLK // tq)
    sub = lambda t: t.reshape(batch, length, dilation * width)
    o, lse = pl.pallas_call(
        functools.partial(_attn_kernel, tq=tq, rblk=rblk),
        grid=(batch, dilation // rblk, length // tq),
        in_specs=[
            _const_spec((2, A_BLK, 2 * A_BLK)),
            pl.BlockSpec((1, tq, rblk * width), lambda b, r, t: (b, t, r)),
            pl.BlockSpec((1, length, rblk * width), lambda b, r, t: (b, 0, r)),
            pl.BlockSpec((1, length, rblk * width), lambda b, r, t: (b, 0, r)),
        ],
        out_specs=[
            pl.BlockSpec((1, tq, rblk * width), lambda b, r, t: (b, t, r)),
            pl.BlockSpec((1, tq, rblk * LANES), lambda b, r, t: (b, t, r)),
        ],
        out_shape=[
            jax.ShapeDtypeStruct((batch, length, dilation * width), BF16),
            jax.ShapeDtypeStruct((batch, length, dilation * LANES), F32),
        ],
        compiler_params=_params("parallel", "parallel", "arbitrary"),
        name=f"attn_d{dilation}",
    )(_band_bias(window // dilation), sub(q), sub(k), sub(v))
    return (o.reshape(batch * length, dilation * width), lse.reshape(batch * length, dilation * LANES))


def _merge_kernel(x_ref, o1_ref, o2_ref, o3_ref, l1_ref, l2_ref, l3_ref, yb_ref, ga_ref, gg_ref,
                  ex_ref, wa_ref, wb_ref, wo_ref, out_ref, oslab_ref, lslab_ref, *, dilations):
    tm = x_ref.shape[0]
    slabs = oslab_ref.shape[1]
    b = _dot(yb_ref[...], wb_ref[...])

    def natural(piece, k, nslab, bufs):
        src = piece
        for level in range(k, 0, -1):
            d, d_prev = dilations[level], dilations[level - 1]
            ratio, n, n_prev = d // d_prev, tm // d, tm // d_prev
            dst = bufs.at[(level - 1) % 2]
            for r in range(d):
                r_prev, j = r % d_prev, r // d_prev
                for c in range(nslab):
                    dst[c, pl.ds(r_prev * n_prev + j, n, stride=ratio), :] = src(r, c)
            src = lambda r, c, dst=dst, n_prev=n_prev: dst[c, pl.ds(r * n_prev, n_prev), :]
        return [bufs[0, c] for c in range(nslab)]

    lses = []
    for k, l_ref in enumerate((l1_ref, l2_ref, l3_ref)):
        if k == 0:
            lses.append(l_ref[...])
        else:
            lses.append(natural(lambda r, c, l_ref=l_ref: l_ref[:, pl.ds(r * LANES, LANES)],
                                k, 1, lslab_ref)[0])
    top = jnp.maximum(jnp.maximum(lses[0], lses[1]), lses[2])
    ws = [jnp.exp2(l - top) for l in lses]
    inv = 1.0 / (ws[0] + ws[1] + ws[2])
    ya = None
    for k, (w, o_ref) in enumerate(zip(ws, (o1_ref, o2_ref, o3_ref))):
        if k == 0:
            o = o_ref[...].astype(F32)
        else:
            o = jnp.concatenate(natural(
                lambda r, c, o_ref=o_ref: o_ref[:, pl.ds((r * slabs + c) * LANES, LANES)].astype(F32),
                k, slabs, oslab_ref), axis=1)
        term = _dot((w * inv).astype(BF16), ex_ref[...]) * o
        ya = term if ya is None else ya + term
    a = _dot(ya.astype(BF16), wa_ref[...])
    merged = ga_ref[...].astype(F32) * a + gg_ref[...].astype(F32) * b
    out_ref[...] = x_ref[...] + _dot(merged.astype(BF16), wo_ref[...])


def _merge_call(x2, os_, lses, yb, ga, gg, wa, wb, wo, *, tm, dilations):
    m, d = x2.shape
    width = yb.shape[1]
    expand = np.zeros((LANES, width), np.float32)
    for h in range(width // A_HEAD_DIM):
        expand[h, h * A_HEAD_DIM:(h + 1) * A_HEAD_DIM] = 1.0
    row = lambda n: pl.BlockSpec((tm, n), lambda i: (i, 0))
    dil = lambda n: [pl.BlockSpec((tm // dl, dl * n), lambda i: (i, 0)) for dl in dilations]
    return pl.pallas_call(
        functools.partial(_merge_kernel, dilations=dilations),
        grid=(m // tm,),
        in_specs=[row(d)] + dil(width) + dil(LANES) + [row(width), row(d), row(d)]
        + [_const_spec((LANES, width)), _const_spec(wa.shape), _const_spec(wb.shape),
           _const_spec(wo.shape)],
        out_specs=row(d),
        out_shape=jax.ShapeDtypeStruct((m, d), F32),
        scratch_shapes=[pltpu.VMEM((2, width // LANES, tm, LANES), F32),
                        pltpu.VMEM((2, 1, tm, LANES), F32)],
        compiler_params=_params("parallel"),
        name="merge",
    )(x2, *os_, *lses, yb, ga, gg, jnp.asarray(expand, BF16), wa, wb, wo)


def _ffn_kernel(x_ref, p_ref, nf_ref, np_ref, nl_ref, wgu_ref, wd_ref, wpe_ref, wpg_ref,
                out_ref, *, chunk):
    x = x_ref[...]
    h = _rms(x, nf_ref[...]).astype(BF16)
    hidden = wd_ref.shape[0]
    acc = x
    for c in range(0, hidden, chunk):
        g = _dot(h, wgu_ref[:, c:c + chunk])
        u = _dot(h, wgu_ref[:, hidden + c:hidden + c + chunk])
        acc = acc + _dot((g * jax.nn.sigmoid(g) * u).astype(BF16), wd_ref[c:c + chunk, :])
    hp = _rms(acc, np_ref[...]).astype(BF16)
    pe = _dot(p_ref[...].astype(BF16), wpe_ref[...])
    x3 = acc + pe * jax.nn.sigmoid(_dot(hp, wpg_ref[...]))
    out_ref[...] = _rms(x3, nl_ref[...])


def _ffn_call(x2, p2, nf, npl, nl, wgu, wd, wpe, wpg, *, tm, chunk):
    m, d = x2.shape
    assert wd.shape[0] % chunk == 0 and wgu.shape[1] == 2 * wd.shape[0]
    row = lambda n: pl.BlockSpec((tm, n), lambda i: (i, 0))
    return pl.pallas_call(
        functools.partial(_ffn_kernel, chunk=chunk),
        grid=(m // tm,),
        in_specs=[row(d), row(p2.shape[1])] + [_const_spec((1, d))] * 3
        + [_const_spec(w.shape) for w in (wgu, wd, wpe, wpg)],
        out_specs=row(d),
        out_shape=jax.ShapeDtypeStruct((m, d), F32),
        compiler_params=_params("parallel"),
        name="ffn",
    )(x2, p2, nf, npl, nl, wgu, wd, wpe, wpg)


def kernel(x, p, norm_mix, w_in, hg_lb, hg_norm, w_a_up, w_b_up, w_out, norm_ffn, w_gu, w_down,
           norm_ple, w_pe, w_pg, norm_final):
    batch, seq, d = x.shape
    depth = p.shape[0]
    assert depth == 1 and hg_lb.shape[0] == 2
    m = batch * seq
    tm = 512
    bf = lambda w: w.astype(BF16)
    vec = lambda g: g.reshape(1, -1).astype(F32)

    dilations = tuple(dl for _, dl in A_PATTERNS)
    x2 = x.reshape(m, d)
    outs = _front_call(x2, vec(norm_mix[0]), bf(w_in[0]), hg_lb.astype(F32), vec(hg_norm[0]),
                       batch=batch, seq=seq, tm=tm, dilations=dilations)
    nd = len(dilations)
    qs, ks, vs = outs[:nd], outs[nd:2 * nd], outs[2 * nd:3 * nd]
    yb, ga, gg = outs[3 * nd:]
    os_, lses = [], []
    for g, (window, dilation) in enumerate(A_PATTERNS):
        o, lse = _attn_call(qs[g], ks[g], vs[g], batch=batch, seq=seq, window=window, dilation=dilation)
        os_.append(o)
        lses.append(lse)
    x2 = _merge_call(x2, os_, lses, yb, ga, gg, bf(w_a_up[0]), bf(w_b_up[0]), bf(w_out[0]), tm=tm,
                     dilations=dilations)
    out = _ffn_call(x2, p[0].reshape(m, -1), vec(norm_ffn[0]), vec(norm_ple[0]), vec(norm_final),
                    bf(w_gu[0]), bf(w_down[0]), bf(w_pe[0]), bf(w_pg[0]), tm=tm, chunk=256)
    return out.reshape(batch, seq, d)
```

```python
import functools

import jax
import jax.numpy as jnp
import numpy as np
from jax import lax
from jax.experimental import pallas as pl
from jax.experimental.pallas import tpu as pltpu

F32 = jnp.float32
BF16 = jnp.bfloat16

LANES = 128
EPS = 1e-6
NEG = -1e30
LOG2E = 1.4426950408889634

A_HEAD_DIM = 64
A_PATTERNS = ((128, 1), (512, 4), (2048, 16))
A_BLK = 128
ATTN_UNITS = 8
B_HEAD_DIM = 128
HG_TILE = 256

VMEM_LIMIT = 56 * 1024 * 1024


def _params(*sem):
    return pltpu.CompilerParams(dimension_semantics=sem, vmem_limit_bytes=VMEM_LIMIT)


def _const_spec(shape):
    nd = len(shape)
    return pl.BlockSpec(shape, lambda *_: (0,) * nd, pipeline_mode=pl.Buffered(1))


def _dot(a, b):
    return jnp.dot(a, b, preferred_element_type=F32)


def _dot_nt(a, b):
    return lax.dot_general(a, b, (((1,), (1,)), ((), ())), preferred_element_type=F32)


def _dot_tn(a, b):
    return lax.dot_general(a, b, (((0,), (0,)), ((), ())), preferred_element_type=F32)


def _rms(x, g):
    return x * lax.rsqrt(jnp.mean(x * x, axis=-1, keepdims=True) + EPS) * g


SUB = 8
HG_BLOCK = 64
HG_SAFE_LOG2 = 100.0


def _rep(t, n):
    return jnp.concatenate([t] * n, axis=0)


def _small_levels(lf2, visit=None):
    sub = lax.broadcasted_iota(jnp.int32, (1, SUB, lf2.shape[-1]), 1)
    pre = tot = lf2.reshape(lf2.shape[0] // SUB, SUB, lf2.shape[-1])
    m, level = 1, 1
    while m < SUB:
        upper = (sub & m) != 0
        if visit is not None:
            visit(level, upper, pre, tot)
        below = pltpu.roll(tot, m, 1)
        above = pltpu.roll(tot, SUB - m, 1)
        pre = pre + jnp.where(upper, below, 0.0)
        tot = tot + jnp.where(upper, below, above)
        m, level = 2 * m, level + 1
    return pre, tot


def _log_decay(lf2):
    sub = lax.broadcasted_iota(jnp.int32, (1, SUB, lf2.shape[-1]), 1)
    pre = lf2.reshape(lf2.shape[0] // SUB, SUB, lf2.shape[-1])
    k = 1
    while k < SUB:
        pre = pre + jnp.where(sub >= k, pltpu.roll(pre, k, 1), 0.0)
        k *= 2
    tot = jnp.broadcast_to(pre[:, SUB - 1:SUB, :], pre.shape)
    ends = [tot[0]]
    for g in range(1, pre.shape[0]):
        ends.append(ends[-1] + tot[g])
    b = jnp.concatenate([pre[0]] + [pre[g] + ends[g - 1] for g in range(1, len(ends))], axis=0)
    return b, ends


def _hgrn_unit(q, kk, lf2, b, ends, v, gact, gn, lvl, state_ref, head, block_start):
    tile = q.shape[0]
    groups, half = tile // SUB, tile // 2
    quads = [slice(r0, r0 + half) for r0 in (0, half)]

    def level_factors(m):
        parts = []
        for lo in range(0, tile, 2 * m):
            mid = _rep(ends[(lo + m) // SUB - 1], m // SUB)
            parts.append(kk[lo:lo + m] * jnp.exp2(mid - b[lo:lo + m]))
            parts.append(q[lo + m:lo + 2 * m] * jnp.exp2(b[lo + m:lo + 2 * m] - mid))
        return jnp.concatenate(parts, axis=0)

    def add_level(level, y):
        yb16 = y.astype(BF16)
        for i, rows in enumerate(quads):
            diag[i] = jnp.where(lvl == level, _dot_nt(yb16[rows], yb16[rows]), diag[i])

    if block_start:
        qs, ks = [], []
        for lo in range(0, tile, HG_BLOCK):
            d = b[lo:lo + HG_BLOCK]
            if lo:
                d = d - _rep(ends[lo // SUB - 1], HG_BLOCK // SUB)
            qs.append(q[lo:lo + HG_BLOCK] * jnp.exp2(d))
            ks.append(kk[lo:lo + HG_BLOCK] * jnp.exp2(-d))
        qf = jnp.concatenate(qs, axis=0).astype(BF16)
        kf = jnp.concatenate(ks, axis=0).astype(BF16)
        diag = [jnp.where(lvl >= 0, _dot_nt(qf[rows], kf[rows]), 0.0) for rows in quads]
        m, level = HG_BLOCK, HG_BLOCK.bit_length()
    else:
        qb16, kb16 = q.astype(BF16), kk.astype(BF16)
        diag = [jnp.where(lvl == 0, _dot_nt(qb16[rows], kb16[rows]), 0.0) for rows in quads]
        g3 = lambda t: t.reshape(groups, SUB, t.shape[-1])
        q3, k3 = g3(q), g3(kk)

        def small(level, upper, pre, tot):
            y3 = jnp.where(upper, q3, k3) * jnp.exp2(jnp.where(upper, pre, tot - pre))
            add_level(level, y3.reshape(tile, -1))

        _small_levels(lf2, small)
        m, level = SUB, SUB.bit_length()
    while m < half:
        add_level(level, level_factors(m))
        m, level = 2 * m, level + 1
    top = level_factors(half).astype(BF16)
    cross = _dot_nt(top[half:], top[:half]).astype(BF16)
    diag = [sc.astype(BF16) for sc in diag]

    st = state_ref[head]
    whole = _rep(ends[-1], groups)
    inter = _dot_nt((q * jnp.exp2(b)).astype(BF16), st.astype(BF16))
    o = inter + jnp.concatenate(
        [_dot(diag[0], v[:half]), _dot(jnp.concatenate([cross, diag[1]], axis=1), v)], axis=0)
    k_end = (kk * jnp.exp2(whole - b)).astype(BF16)
    state_ref[head] = st * jnp.exp2(ends[-1][0:1, :]) + _dot_tn(v, k_end)
    return _rms(o, gn) * gact


def _front_kernel(lvl_ref, hglb_ref, gn_ref, x_ref, g_ref, w_ref, *refs, width, dilations):
    nd = len(dilations)
    attn_refs = refs[:3 * nd]
    yb_ref, ga_ref, gg_ref, hx_ref, slab_ref, state_ref = refs[3 * nd:]
    tm = x_ref.shape[0]
    heads = width // B_HEAD_DIM

    @pl.when(pl.program_id(1) == 0)
    def _():
        state_ref[...] = jnp.zeros_like(state_ref)

    hx_ref[...] = _rms(x_ref[...], g_ref[...]).astype(BF16)
    half = tm // 2
    slabs = width // LANES

    def proj(t0, rows, c0, n):
        return _dot(hx_ref[pl.ds(t0, rows), :], w_ref[:, c0:c0 + n])

    def attn_job(a, t0):
        y = proj(t0, half, a * width, width)
        if a == 0:
            y = y * (A_HEAD_DIM ** -0.5 * LOG2E)
        for c in range(slabs):
            slab_ref[0, a * slabs + c, pl.ds(t0, half), :] = y[:, c * LANES:(c + 1) * LANES]
        d_prev = 1
        for k, (out_ref, d) in enumerate(zip(attn_refs[a * nd:(a + 1) * nd], dilations)):
            if d == 1:
                out_ref[pl.ds(t0, half), :] = y.astype(BF16)
                continue
            ratio, n_prev, n = d // d_prev, half // d_prev, half // d
            for r in range(d):
                r_prev, j = r % d_prev, r // d_prev
                for c in range(slabs):
                    piece = slab_ref[k - 1, a * slabs + c,
                                     pl.ds(t0 + r_prev * n_prev + j, n, stride=ratio), :]
                    out_ref[pl.ds(t0 // d, n), pl.ds((r * slabs + c) * LANES, LANES)] = piece.astype(BF16)
                    if k + 1 < nd:
                        slab_ref[k, a * slabs + c, pl.ds(t0 + r * n, n), :] = piece
            d_prev = d

    def gate_job(out_ref, c0, j, t0):
        y = proj(t0, half, c0 + j * width, width)
        out_ref[pl.ds(t0, half), pl.ds(j * width, width)] = jax.nn.sigmoid(y).astype(BF16)

    jobs = [functools.partial(attn_job, a, t0) for a in range(3) for t0 in (0, half)]
    jobs += [functools.partial(gate_job, ref, c0, j, t0)
             for ref, c0 in ((ga_ref, 7 * width), (gg_ref, 9 * width)) for j in range(2)
             for t0 in (0, half)]

    hg = hglb_ref[...]
    ex = jnp.exp(hg - jnp.max(hg, axis=0, keepdims=True))
    lb_all = ex[0:1] / jnp.sum(ex, axis=0, keepdims=True)
    lvl = lvl_ref[...]

    def head_proj(hd):
        hp = proj(0, tm, 3 * width + 4 * hd * B_HEAD_DIM, 4 * B_HEAD_DIM)
        q, fgate, v, gate = (hp[:, j * B_HEAD_DIM:(j + 1) * B_HEAD_DIM] for j in range(4))
        lb = lb_all[:, hd * B_HEAD_DIM:(hd + 1) * B_HEAD_DIM]
        f = lb + (1.0 - lb) * jax.nn.sigmoid(fgate)
        return q, 1.0 - f, jnp.log2(f), v.astype(BF16), gate * jax.nn.sigmoid(gate)

    units, worst = [], None
    for hd in range(heads):
        data = head_proj(hd)
        for t0 in range(0, tm, HG_TILE):
            q, kk, lf2, v, gact = (t[t0:t0 + HG_TILE] for t in data)
            b, ends = _log_decay(lf2)
            for lo in range(0, HG_TILE, HG_BLOCK):
                blk = ends[(lo + HG_BLOCK) // SUB - 1]
                if lo:
                    blk = blk - ends[lo // SUB - 1]
                worst = blk if worst is None else jnp.minimum(worst, blk)
            units.append((hd, t0, q, kk, lf2, b, ends, v, gact))
    for job in jobs:
        job()

    def run(block_start):
        for hd, t0, *args in units:
            y = _hgrn_unit(*args, gn_ref[...], lvl, state_ref, hd, block_start)
            yb_ref[pl.ds(t0, HG_TILE), pl.ds(hd * B_HEAD_DIM, B_HEAD_DIM)] = y.astype(yb_ref.dtype)

    safe = jnp.min(worst) >= -HG_SAFE_LOG2
    pl.when(safe)(functools.partial(run, True))
    pl.when(jnp.logical_not(safe))(functools.partial(run, False))


def _level_map(tile):
    t = np.arange(tile)[:, None]
    s = np.arange(tile)[None, :]
    x = np.maximum(t ^ s, 1)
    lv = np.floor(np.log2(x)).astype(np.int32) + 1
    return jnp.asarray(np.where(s < t, lv, np.where(s == t, 0, -1)), jnp.int32)


def _front_call(x2, g, w, hg_lb, g_norm, *, batch, seq, tm, dilations):
    m, d = x2.shape
    width = d // 2
    assert seq % tm == 0 and tm % HG_TILE == 0
    assert dilations[0] == 1 and all(b % a == 0 for a, b in zip(dilations, dilations[1:]))
    steps = seq // tm
    row = lambda n: pl.BlockSpec((tm, n), lambda b, t: (b * steps + t, 0))
    sds = lambda n, dt: jax.ShapeDtypeStruct((m, n), dt)
    dil_specs = [pl.BlockSpec((tm // dl, dl * width), lambda b, t: (b * steps + t, 0))
                 for dl in dilations] * 3
    dil_shapes = [jax.ShapeDtypeStruct((m // dl, dl * width), BF16) for dl in dilations] * 3
    return pl.pallas_call(
        functools.partial(_front_kernel, width=width, dilations=dilations),
        grid=(batch, steps),
        in_specs=[_const_spec((HG_TILE // 2, HG_TILE // 2)), _const_spec(hg_lb.shape),
                  _const_spec((1, B_HEAD_DIM)), row(d), _const_spec((1, d)), _const_spec(w.shape)],
        out_specs=dil_specs + [row(width), row(d), row(d)],
        out_shape=dil_shapes + [sds(width, BF16), sds(d, BF16), sds(d, BF16)],
        scratch_shapes=[pltpu.VMEM((tm, d), BF16),
                        pltpu.VMEM((len(dilations) - 1, 3 * width // LANES, tm, LANES), F32),
                        pltpu.VMEM((width // B_HEAD_DIM, B_HEAD_DIM, B_HEAD_DIM), F32)],
        compiler_params=_params("parallel", "arbitrary"),
        name="front",
    )(_level_map(HG_TILE // 2), hg_lb, g_norm, x2, g, _head_major(w, width))


def _head_major(w, width):
    heads = width // B_HEAD_DIM
    hg = w[:, 3 * width:7 * width].reshape(w.shape[0], 4, heads, B_HEAD_DIM)
    hg = hg.transpose(0, 2, 1, 3).reshape(w.shape[0], 4 * width)
    return jnp.concatenate([w[:, :3 * width], hg, w[:, 7 * width:]], axis=1)


def _attn_kernel(bias_ref, q_ref, k_ref, v_ref, o_ref, lse_ref, *, tq, rblk):
    width = q_ref.shape[2] // rblk
    pairs = width // LANES
    lane = lax.broadcasted_iota(jnp.int32, (A_BLK, LANES), 1)
    low_half = lane < A_HEAD_DIM
    ones = jnp.ones((2 * A_BLK, LANES), BF16)
    for rr in range(rblk):
        for qi in range(tq // A_BLK):
            n = pl.program_id(2) * (tq // A_BLK) + qi
            start = pl.multiple_of(jnp.maximum(n - 1, 0) * A_BLK, A_BLK)
            bias = bias_ref[jnp.where(n == 0, 1, 0)]
            rows = pl.ds(qi * A_BLK, A_BLK)
            mx_tile = jnp.zeros((A_BLK, LANES), F32)
            den_tile = jnp.ones((A_BLK, LANES), F32)
            for p in range(pairs):
                cols = pl.ds(rr * width + p * LANES, LANES)
                q_pair = q_ref[0, rows, cols]
                k_band = k_ref[0, pl.ds(start, 2 * A_BLK), cols]
                v_cat = jnp.concatenate([v_ref[0, pl.ds(start, 2 * A_BLK), cols], ones], axis=1)
                nums, dens = [], []
                for hh in range(2):
                    mine = low_half if hh == 0 else jnp.logical_not(low_half)
                    s = _dot_nt(jnp.where(mine, q_pair, jnp.zeros_like(q_pair)), k_band) + bias
                    mx = jnp.max(s, axis=-1, keepdims=True)
                    r = _dot(jnp.exp2(s - mx).astype(BF16), v_cat)
                    nums.append(r[:, :LANES])
                    dens.append(r[:, LANES:])
                    mx_tile = jnp.where(lane == 2 * p + hh, mx, mx_tile)
                    den_tile = jnp.where(lane == 2 * p + hh, dens[-1], den_tile)
                o_ref[0, rows, cols] = (jnp.where(low_half, nums[0], nums[1])
                                        / jnp.where(low_half, dens[0], dens[1])).astype(o_ref.dtype)
            lse_ref[0, rows, pl.ds(rr * LANES, LANES)] = mx_tile + jnp.log2(den_tile)


def _band_bias(span):
    a = np.arange(A_BLK)[:, None]
    c = np.arange(2 * A_BLK)[None, :]
    rel = a + A_BLK - c
    normal = (rel >= 0) & (rel <= span)
    rel0 = a - c
    first = (rel0 >= 0) & (rel0 <= span)
    return jnp.asarray(np.where(np.stack([normal, first]), 0.0, NEG), F32)


def _attn_call(q, k, v, *, batch, seq, window, dilation):
    width = q.shape[1] // dilation
    assert seq % (dilation * A_BLK) == 0 and window % dilation == 0
    length = seq // dilation
    tq = min(length, ATTN_UNITS * A_BLK)
    rblk = min(dilation, ATTN_UNITS * A_BLK // tq)
    sub = lambda t: t.reshape(batch, length, dilation * width)
    o, lse = pl.pallas_call(
        functools.partial(_attn_kernel, tq=tq, rblk=rblk),
        grid=(batch, dilation // rblk, length // tq),
        in_specs=[
            _const_spec((2, A_BLK, 2 * A_BLK)),
            pl.BlockSpec((1, tq, rblk * width), lambda b, r, t: (b, t, r)),
            pl.BlockSpec((1, length, rblk * width), lambda b, r, t: (b, 0, r)),
            pl.BlockSpec((1, length, rblk * width), lambda b, r, t: (b, 0, r)),
        ],
        out_specs=[
            pl.BlockSpec((1, tq, rblk * width), lambda b, r, t: (b, t, r)),
            pl.BlockSpec((1, tq, rblk * LANES), lambda b, r, t: (b, t, r)),
        ],
        out_shape=[
            jax.ShapeDtypeStruct((batch, length, dilation * width), BF16),
            jax.ShapeDtypeStruct((batch, length, dilation * LANES), F32),
        ],
        compiler_params=_params("parallel", "parallel", "arbitrary"),
        name=f"attn_d{dilation}",
    )(_band_bias(window // dilation), sub(q), sub(k), sub(v))
    return (o.reshape(batch * length, dilation * width), lse.reshape(batch * length, dilation * LANES))


def _merge_body(x_ref, o1_ref, o2_ref, o3_ref, l1_ref, l2_ref, l3_ref, yb_ref, ga_ref, gg_ref,
                ex_ref, wa_ref, wb_ref, wo_ref, oslab_ref, lslab_ref, dilations):
    tm = x_ref.shape[0]
    slabs = oslab_ref.shape[1]
    b = _dot(yb_ref[...], wb_ref[...])

    def natural(piece, k, nslab, bufs):
        src = piece
        for level in range(k, 0, -1):
            d, d_prev = dilations[level], dilations[level - 1]
            ratio, n, n_prev = d // d_prev, tm // d, tm // d_prev
            dst = bufs.at[(level - 1) % 2]
            for r in range(d):
                r_prev, j = r % d_prev, r // d_prev
                for c in range(nslab):
                    dst[c, pl.ds(r_prev * n_prev + j, n, stride=ratio), :] = src(r, c)
            src = lambda r, c, dst=dst, n_prev=n_prev: dst[c, pl.ds(r * n_prev, n_prev), :]
        return [bufs[0, c] for c in range(nslab)]

    lses = []
    for k, l_ref in enumerate((l1_ref, l2_ref, l3_ref)):
        if k == 0:
            lses.append(l_ref[...])
        else:
            lses.append(natural(lambda r, c, l_ref=l_ref: l_ref[:, pl.ds(r * LANES, LANES)],
                                k, 1, lslab_ref)[0])
    top = jnp.maximum(jnp.maximum(lses[0], lses[1]), lses[2])
    ws = [jnp.exp2(l - top) for l in lses]
    inv = 1.0 / (ws[0] + ws[1] + ws[2])
    ya = None
    for k, (w, o_ref) in enumerate(zip(ws, (o1_ref, o2_ref, o3_ref))):
        if k == 0:
            o = o_ref[...].astype(F32)
        else:
            o = jnp.concatenate(natural(
                lambda r, c, o_ref=o_ref: o_ref[:, pl.ds((r * slabs + c) * LANES, LANES)].astype(F32),
                k, slabs, oslab_ref), axis=1)
        term = _dot((w * inv).astype(BF16), ex_ref[...]) * o
        ya = term if ya is None else ya + term
    a = _dot(ya.astype(BF16), wa_ref[...])
    merged = ga_ref[...].astype(F32) * a + gg_ref[...].astype(F32) * b
    return x_ref[...] + _dot(merged.astype(BF16), wo_ref[...])


def _ffn_body(x, p_ref, nf_ref, np_ref, nl_ref, wgu_ref, wd_ref, wpe_ref, wpg_ref, chunk):
    h = _rms(x, nf_ref[...]).astype(BF16)
    hidden = wd_ref.shape[0]
    acc = x
    for c in range(0, hidden, chunk):
        g = _dot(h, wgu_ref[:, c:c + chunk])
        u = _dot(h, wgu_ref[:, hidden + c:hidden + c + chunk])
        acc = acc + _dot((g * jax.nn.sigmoid(g) * u).astype(BF16), wd_ref[c:c + chunk, :])
    hp = _rms(acc, np_ref[...]).astype(BF16)
    pe = _dot(p_ref[...].astype(BF16), wpe_ref[...])
    x3 = acc + pe * jax.nn.sigmoid(_dot(hp, wpg_ref[...]))
    return _rms(x3, nl_ref[...])


def _back_kernel(*refs, dilations, chunk):
    rows, p_ref, consts, norms, ffn_w = refs[:10], refs[10], refs[11:15], refs[15:18], refs[18:22]
    out_ref, oslab_ref, lslab_ref = refs[22:]
    x1 = _merge_body(*rows, *consts, oslab_ref, lslab_ref, dilations)
    out_ref[...] = _ffn_body(x1, p_ref, *norms, *ffn_w, chunk)


def _back_call(x2, os_, lses, yb, ga, gg, p2, wa, wb, wo, nf, npl, nl, wgu, wd, wpe, wpg, *, tm,
               dilations, chunk):
    m, d = x2.shape
    width = yb.shape[1]
    assert wd.shape[0] % chunk == 0 and wgu.shape[1] == 2 * wd.shape[0]
    expand = np.zeros((LANES, width), np.float32)
    for h in range(width // A_HEAD_DIM):
        expand[h, h * A_HEAD_DIM:(h + 1) * A_HEAD_DIM] = 1.0
    expand = jnp.asarray(expand, BF16)
    row = lambda n: pl.BlockSpec((tm, n), lambda i: (i, 0))
    dil = lambda n: [pl.BlockSpec((tm // dl, dl * n), lambda i: (i, 0)) for dl in dilations]
    consts = (expand, wa, wb, wo, nf, npl, nl, wgu, wd, wpe, wpg)
    return pl.pallas_call(
        functools.partial(_back_kernel, dilations=dilations, chunk=chunk),
        grid=(m // tm,),
        in_specs=[row(d)] + dil(width) + dil(LANES) + [row(width), row(d), row(d), row(p2.shape[1])]
        + [_const_spec(c.shape) for c in consts],
        out_specs=row(d),
        out_shape=jax.ShapeDtypeStruct((m, d), F32),
        scratch_shapes=[pltpu.VMEM((2, width // LANES, tm, LANES), F32),
                        pltpu.VMEM((2, 1, tm, LANES), F32)],
        compiler_params=_params("parallel"),
        name="back",
    )(x2, *os_, *lses, yb, ga, gg, p2, *consts)


def kernel(x, p, norm_mix, w_in, hg_lb, hg_norm, w_a_up, w_b_up, w_out, norm_ffn, w_gu, w_down,
           norm_ple, w_pe, w_pg, norm_final):
    batch, seq, d = x.shape
    depth = p.shape[0]
    assert depth == 1 and hg_lb.shape[0] == 2
    m = batch * seq
    tm = 512
    bf = lambda w: w.astype(BF16)
    vec = lambda g: g.reshape(1, -1).astype(F32)

    dilations = tuple(dl for _, dl in A_PATTERNS)
    x2 = x.reshape(m, d)
    outs = _front_call(x2, vec(norm_mix[0]), bf(w_in[0]), hg_lb.astype(F32), vec(hg_norm[0]),
                       batch=batch, seq=seq, tm=tm, dilations=dilations)
    nd = len(dilations)
    qs, ks, vs = outs[:nd], outs[nd:2 * nd], outs[2 * nd:3 * nd]
    yb, ga, gg = outs[3 * nd:]
    os_, lses = [], []
    for g, (window, dilation) in enumerate(A_PATTERNS):
        o, lse = _attn_call(qs[g], ks[g], vs[g], batch=batch, seq=seq, window=window, dilation=dilation)
        os_.append(o)
        lses.append(lse)
    out = _back_call(x2, os_, lses, yb, ga, gg, p[0].reshape(m, -1), bf(w_a_up[0]), bf(w_b_up[0]),
                     bf(w_out[0]), vec(norm_ffn[0]), vec(norm_ple[0]), vec(norm_final), bf(w_gu[0]),
                     bf(w_down[0]), bf(w_pe[0]), bf(w_pg[0]), tm=tm, dilations=dilations, chunk=256)
    return out.reshape(batch, seq, d)
```

```python
import functools

import jax
import jax.numpy as jnp
import numpy as np
from jax import lax
from jax.experimental import pallas as pl
from jax.experimental.pallas import tpu as pltpu

F32 = jnp.float32
BF16 = jnp.bfloat16

LANES = 128
EPS = 1e-6
NEG = -1e30
LOG2E = 1.4426950408889634

A_HEAD_DIM = 64
A_PATTERNS = ((128, 1), (512, 4), (2048, 16))
A_BLK = 128
ATTN_UNITS = 8
B_HEAD_DIM = 128
HG_TILE = 256

VMEM_LIMIT = 56 * 1024 * 1024


def _params(*sem):
    return pltpu.CompilerParams(dimension_semantics=sem, vmem_limit_bytes=VMEM_LIMIT)


def _const_spec(shape):
    nd = len(shape)
    return pl.BlockSpec(shape, lambda *_: (0,) * nd, pipeline_mode=pl.Buffered(1))


def _dot(a, b):
    return jnp.dot(a, b, preferred_element_type=F32)


def _dot_nt(a, b):
    return lax.dot_general(a, b, (((1,), (1,)), ((), ())), preferred_element_type=F32)


def _dot_tn(a, b):
    return lax.dot_general(a, b, (((0,), (0,)), ((), ())), preferred_element_type=F32)


def _rms(x, g):
    return x * lax.rsqrt(jnp.mean(x * x, axis=-1, keepdims=True) + EPS) * g


SUB = 8
HG_BLOCK = 64
HG_SAFE_LOG2 = 100.0


def _rep(t, n):
    return jnp.concatenate([t] * n, axis=0)


def _small_levels(lf2, visit=None):
    sub = lax.broadcasted_iota(jnp.int32, (1, SUB, lf2.shape[-1]), 1)
    pre = tot = lf2.reshape(lf2.shape[0] // SUB, SUB, lf2.shape[-1])
    m, level = 1, 1
    while m < SUB:
        upper = (sub & m) != 0
        if visit is not None:
            visit(level, upper, pre, tot)
        below = pltpu.roll(tot, m, 1)
        above = pltpu.roll(tot, SUB - m, 1)
        pre = pre + jnp.where(upper, below, 0.0)
        tot = tot + jnp.where(upper, below, above)
        m, level = 2 * m, level + 1
    return pre, tot


def _log_decay(lf2):
    sub = lax.broadcasted_iota(jnp.int32, (1, SUB, lf2.shape[-1]), 1)
    pre = lf2.reshape(lf2.shape[0] // SUB, SUB, lf2.shape[-1])
    k = 1
    while k < SUB:
        pre = pre + jnp.where(sub >= k, pltpu.roll(pre, k, 1), 0.0)
        k *= 2
    tot = jnp.broadcast_to(pre[:, SUB - 1:SUB, :], pre.shape)
    ends = [tot[0]]
    for g in range(1, pre.shape[0]):
        ends.append(ends[-1] + tot[g])
    b = jnp.concatenate([pre[0]] + [pre[g] + ends[g - 1] for g in range(1, len(ends))], axis=0)
    return b, ends


def _hgrn_unit(q, kk, lf2, b, ends, v, gact, gn, lvl, state_ref, head, block_start):
    tile = q.shape[0]
    groups, half = tile // SUB, tile // 2
    quads = [slice(r0, r0 + half) for r0 in (0, half)]

    def level_factors(m):
        parts = []
        for lo in range(0, tile, 2 * m):
            mid = _rep(ends[(lo + m) // SUB - 1], m // SUB)
            parts.append(kk[lo:lo + m] * jnp.exp2(mid - b[lo:lo + m]))
            parts.append(q[lo + m:lo + 2 * m] * jnp.exp2(b[lo + m:lo + 2 * m] - mid))
        return jnp.concatenate(parts, axis=0)

    def add_level(level, y):
        yb16 = y.astype(BF16)
        for i, rows in enumerate(quads):
            diag[i] = jnp.where(lvl == level, _dot_nt(yb16[rows], yb16[rows]), diag[i])

    if block_start:
        qs, ks = [], []
        for lo in range(0, tile, HG_BLOCK):
            d = b[lo:lo + HG_BLOCK]
            if lo:
                d = d - _rep(ends[lo // SUB - 1], HG_BLOCK // SUB)
            qs.append(q[lo:lo + HG_BLOCK] * jnp.exp2(d))
            ks.append(kk[lo:lo + HG_BLOCK] * jnp.exp2(-d))
        qf = jnp.concatenate(qs, axis=0).astype(BF16)
        kf = jnp.concatenate(ks, axis=0).astype(BF16)
        diag = [jnp.where(lvl >= 0, _dot_nt(qf[rows], kf[rows]), 0.0) for rows in quads]
        m, level = HG_BLOCK, HG_BLOCK.bit_length()
    else:
        qb16, kb16 = q.astype(BF16), kk.astype(BF16)
        diag = [jnp.where(lvl == 0, _dot_nt(qb16[rows], kb16[rows]), 0.0) for rows in quads]
        g3 = lambda t: t.reshape(groups, SUB, t.shape[-1])
        q3, k3 = g3(q), g3(kk)

        def small(level, upper, pre, tot):
            y3 = jnp.where(upper, q3, k3) * jnp.exp2(jnp.where(upper, pre, tot - pre))
            add_level(level, y3.reshape(tile, -1))

        _small_levels(lf2, small)
        m, level = SUB, SUB.bit_length()
    while m < half:
        add_level(level, level_factors(m))
        m, level = 2 * m, level + 1
    top = level_factors(half).astype(BF16)
    cross = _dot_nt(top[half:], top[:half]).astype(BF16)
    diag = [sc.astype(BF16) for sc in diag]

    st = state_ref[head]
    whole = _rep(ends[-1], groups)
    inter = _dot_nt((q * jnp.exp2(b)).astype(BF16), st.astype(BF16))
    o = inter + jnp.concatenate(
        [_dot(diag[0], v[:half]), _dot(jnp.concatenate([cross, diag[1]], axis=1), v)], axis=0)
    k_end = (kk * jnp.exp2(whole - b)).astype(BF16)
    state_ref[head] = st * jnp.exp2(ends[-1][0:1, :]) + _dot_tn(v, k_end)
    return _rms(o, gn) * gact


def _front_kernel(lvl_ref, hglb_ref, gn_ref, x_ref, g_ref, w_ref, *refs, width, dilations):
    nd = len(dilations)
    attn_refs = refs[:3 * nd]
    yb_ref, ga_ref, gg_ref, hx_ref, slab_ref, state_ref = refs[3 * nd:]
    tm = x_ref.shape[0]
    heads = width // B_HEAD_DIM

    @pl.when(pl.program_id(1) == 0)
    def _():
        state_ref[...] = jnp.zeros_like(state_ref)

    hx_ref[...] = _rms(x_ref[...], g_ref[...]).astype(BF16)
    half = tm // 2
    slabs = width // LANES

    def proj(t0, rows, c0, n):
        return _dot(hx_ref[pl.ds(t0, rows), :], w_ref[:, c0:c0 + n])

    def attn_job(a, t0):
        y = proj(t0, half, a * width, width)
        if a == 0:
            y = y * (A_HEAD_DIM ** -0.5 * LOG2E)
        for c in range(slabs):
            slab_ref[0, a * slabs + c, pl.ds(t0, half), :] = y[:, c * LANES:(c + 1) * LANES]
        d_prev = 1
        for k, (out_ref, d) in enumerate(zip(attn_refs[a * nd:(a + 1) * nd], dilations)):
            if d == 1:
                out_ref[pl.ds(t0, half), :] = y.astype(BF16)
                continue
            ratio, n_prev, n = d // d_prev, half // d_prev, half // d
            for r in range(d):
                r_prev, j = r % d_prev, r // d_prev
                for c in range(slabs):
                    piece = slab_ref[k - 1, a * slabs + c,
                                     pl.ds(t0 + r_prev * n_prev + j, n, stride=ratio), :]
                    out_ref[pl.ds(t0 // d, n), pl.ds((r * slabs + c) * LANES, LANES)] = piece.astype(BF16)
                    if k + 1 < nd:
                        slab_ref[k, a * slabs + c, pl.ds(t0 + r * n, n), :] = piece
            d_prev = d

    def gate_job(out_ref, c0, j, t0):
        y = proj(t0, half, c0 + j * width, width)
        out_ref[pl.ds(t0, half), pl.ds(j * width, width)] = jax.nn.sigmoid(y).astype(BF16)

    jobs = [functools.partial(attn_job, a, t0) for a in range(3) for t0 in (0, half)]
    jobs += [functools.partial(gate_job, ref, c0, j, t0)
             for ref, c0 in ((ga_ref, 7 * width), (gg_ref, 9 * width)) for j in range(2)
             for t0 in (0, half)]

    hg = hglb_ref[...]
    ex = jnp.exp(hg - jnp.max(hg, axis=0, keepdims=True))
    lb_all = ex[0:1] / jnp.sum(ex, axis=0, keepdims=True)
    lvl = lvl_ref[...]

    def head_proj(hd):
        w_head = jnp.concatenate(
            [w_ref[:, pl.ds((3 + j) * width + hd * B_HEAD_DIM, B_HEAD_DIM)] for j in range(4)], axis=1)
        hp = _dot(hx_ref[...], w_head)
        q, fgate, v, gate = (hp[:, j * B_HEAD_DIM:(j + 1) * B_HEAD_DIM] for j in range(4))
        lb = lb_all[:, hd * B_HEAD_DIM:(hd + 1) * B_HEAD_DIM]
        f = lb + (1.0 - lb) * jax.nn.sigmoid(fgate)
        return q, 1.0 - f, jnp.log2(f), v.astype(BF16), gate * jax.nn.sigmoid(gate)

    units, worst = [], None
    for hd in range(heads):
        data = head_proj(hd)
        for t0 in range(0, tm, HG_TILE):
            q, kk, lf2, v, gact = (t[t0:t0 + HG_TILE] for t in data)
            b, ends = _log_decay(lf2)
            for lo in range(0, HG_TILE, HG_BLOCK):
                blk = ends[(lo + HG_BLOCK) // SUB - 1]
                if lo:
                    blk = blk - ends[lo // SUB - 1]
                worst = blk if worst is None else jnp.minimum(worst, blk)
            units.append((hd, t0, q, kk, lf2, b, ends, v, gact))
    for job in jobs:
        job()

    def run(block_start):
        for hd, t0, *args in units:
            y = _hgrn_unit(*args, gn_ref[...], lvl, state_ref, hd, block_start)
            yb_ref[pl.ds(t0, HG_TILE), pl.ds(hd * B_HEAD_DIM, B_HEAD_DIM)] = y.astype(yb_ref.dtype)

    safe = jnp.min(worst) >= -HG_SAFE_LOG2
    pl.when(safe)(functools.partial(run, True))
    pl.when(jnp.logical_not(safe))(functools.partial(run, False))


def _level_map(tile):
    t = np.arange(tile)[:, None]
    s = np.arange(tile)[None, :]
    x = np.maximum(t ^ s, 1)
    lv = np.floor(np.log2(x)).astype(np.int32) + 1
    return jnp.asarray(np.where(s < t, lv, np.where(s == t, 0, -1)), jnp.int32)


def _front_call(x2, g, w, hg_lb, g_norm, *, batch, seq, tm, dilations):
    m, d = x2.shape
    width = d // 2
    assert seq % tm == 0 and tm % HG_TILE == 0
    assert dilations[0] == 1 and all(b % a == 0 for a, b in zip(dilations, dilations[1:]))
    steps = seq // tm
    row = lambda n: pl.BlockSpec((tm, n), lambda b, t: (b * steps + t, 0))
    sds = lambda n, dt: jax.ShapeDtypeStruct((m, n), dt)
    dil_specs = [pl.BlockSpec((tm // dl, dl * width), lambda b, t: (b * steps + t, 0))
                 for dl in dilations] * 3
    dil_shapes = [jax.ShapeDtypeStruct((m // dl, dl * width), BF16) for dl in dilations] * 3
    return pl.pallas_call(
        functools.partial(_front_kernel, width=width, dilations=dilations),
        grid=(batch, steps),
        in_specs=[_const_spec((HG_TILE // 2, HG_TILE // 2)), _const_spec(hg_lb.shape),
                  _const_spec((1, B_HEAD_DIM)), row(d), _const_spec((1, d)), _const_spec(w.shape)],
        out_specs=dil_specs + [row(width), row(d), row(d)],
        out_shape=dil_shapes + [sds(width, BF16), sds(d, BF16), sds(d, BF16)],
        scratch_shapes=[pltpu.VMEM((tm, d), BF16),
                        pltpu.VMEM((len(dilations) - 1, 3 * width // LANES, tm, LANES), F32),
                        pltpu.VMEM((width // B_HEAD_DIM, B_HEAD_DIM, B_HEAD_DIM), F32)],
        compiler_params=_params("parallel", "arbitrary"),
        name="front",
    )(_level_map(HG_TILE // 2), hg_lb, g_norm, x2, g, w)


def _attn_kernel(bias_ref, q_ref, k_ref, v_ref, o_ref, lse_ref, *, tq, rblk):
    width = q_ref.shape[2] // rblk
    pairs = width // LANES
    lane = lax.broadcasted_iota(jnp.int32, (A_BLK, LANES), 1)
    low_half = lane < A_HEAD_DIM
    ones = jnp.ones((2 * A_BLK, LANES), BF16)
    for rr in range(rblk):
        for qi in range(tq // A_BLK):
            n = pl.program_id(2) * (tq // A_BLK) + qi
            start = pl.multiple_of(jnp.maximum(n - 1, 0) * A_BLK, A_BLK)
            bias = bias_ref[jnp.where(n == 0, 1, 0)]
            rows = pl.ds(qi * A_BLK, A_BLK)
            mx_tile = jnp.zeros((A_BLK, LANES), F32)
            den_tile = jnp.ones((A_BLK, LANES), F32)
            for p in range(pairs):
                cols = pl.ds(rr * width + p * LANES, LANES)
                q_pair = q_ref[0, rows, cols]
                k_band = k_ref[0, pl.ds(start, 2 * A_BLK), cols]
                v_cat = jnp.concatenate([v_ref[0, pl.ds(start, 2 * A_BLK), cols], ones], axis=1)
                nums, dens = [], []
                for hh in range(2):
                    mine = low_half if hh == 0 else jnp.logical_not(low_half)
                    s = _dot_nt(jnp.where(mine, q_pair, jnp.zeros_like(q_pair)), k_band) + bias
                    mx = jnp.max(s, axis=-1, keepdims=True)
                    r = _dot(jnp.exp2(s - mx).astype(BF16), v_cat)
                    nums.append(r[:, :LANES])
                    dens.append(r[:, LANES:])
                    mx_tile = jnp.where(lane == 2 * p + hh, mx, mx_tile)
                    den_tile = jnp.where(lane == 2 * p + hh, dens[-1], den_tile)
                o_ref[0, rows, cols] = (jnp.where(low_half, nums[0], nums[1])
                                        / jnp.where(low_half, dens[0], dens[1])).astype(o_ref.dtype)
            lse_ref[0, rows, pl.ds(rr * LANES, LANES)] = mx_tile + jnp.log2(den_tile)


def _band_bias(span):
    a = np.arange(A_BLK)[:, None]
    c = np.arange(2 * A_BLK)[None, :]
    rel = a + A_BLK - c
    normal = (rel >= 0) & (rel <= span)
    rel0 = a - c
    first = (rel0 >= 0) & (rel0 <= span)
    return jnp.asarray(np.where(np.stack([normal, first]), 0.0, NEG), F32)


def _attn_call(q, k, v, *, batch, seq, window, dilation):
    width = q.shape[1] // dilation
    assert seq % (dilation * A_BLK) == 0 and window % dilation == 0
    length = seq // dilation
    tq = min(length, ATTN_UNITS * A_BLK)
    rblk = min(dilation, ATTN_UNITS * A_BLK // tq)
    sub = lambda t: t.reshape(batch, length, dilation * width)
    o, lse = pl.pallas_call(
        functools.partial(_attn_kernel, tq=tq, rblk=rblk),
        grid=(batch, dilation // rblk, length // tq),
        in_specs=[
            _const_spec((2, A_BLK, 2 * A_BLK)),
            pl.BlockSpec((1, tq, rblk * width), lambda b, r, t: (b, t, r)),
            pl.BlockSpec((1, length, rblk * width), lambda b, r, t: (b, 0, r)),
            pl.BlockSpec((1, length, rblk * width), lambda b, r, t: (b, 0, r)),
        ],
        out_specs=[
            pl.BlockSpec((1, tq, rblk * width), lambda b, r, t: (b, t, r)),
            pl.BlockSpec((1, tq, rblk * LANES), lambda b, r, t: (b, t, r)),
        ],
        out_shape=[
            jax.ShapeDtypeStruct((batch, length, dilation * width), BF16),
            jax.ShapeDtypeStruct((batch, length, dilation * LANES), F32),
        ],
        compiler_params=_params("parallel", "parallel", "arbitrary"),
        name=f"attn_d{dilation}",
    )(_band_bias(window // dilation), sub(q), sub(k), sub(v))
    return (o.reshape(batch * length, dilation * width), lse.reshape(batch * length, dilation * LANES))


def _merge_body(x_ref, o1_ref, o2_ref, o3_ref, l1_ref, l2_ref, l3_ref, yb_ref, ga_ref, gg_ref,
                ex_ref, wa_ref, wb_ref, wo_ref, oslab_ref, lslab_ref, dilations):
    tm = x_ref.shape[0]
    slabs = oslab_ref.shape[1]
    b = _dot(yb_ref[...], wb_ref[...])

    def natural(piece, k, nslab, bufs):
        src = piece
        for level in range(k, 0, -1):
            d, d_prev = dilations[level], dilations[level - 1]
            ratio, n, n_prev = d // d_prev, tm // d, tm // d_prev
            dst = bufs.at[(level - 1) % 2]
            for r in range(d):
                r_prev, j = r % d_prev, r // d_prev
                for c in range(nslab):
                    dst[c, pl.ds(r_prev * n_prev + j, n, stride=ratio), :] = src(r, c)
            src = lambda r, c, dst=dst, n_prev=n_prev: dst[c, pl.ds(r * n_prev, n_prev), :]
        return [bufs[0, c] for c in range(nslab)]

    lses = []
    for k, l_ref in enumerate((l1_ref, l2_ref, l3_ref)):
        if k == 0:
            lses.append(l_ref[...])
        else:
            lses.append(natural(lambda r, c, l_ref=l_ref: l_ref[:, pl.ds(r * LANES, LANES)],
                                k, 1, lslab_ref)[0])
    top = jnp.maximum(jnp.maximum(lses[0], lses[1]), lses[2])
    ws = [jnp.exp2(l - top) for l in lses]
    inv = 1.0 / (ws[0] + ws[1] + ws[2])
    ya = None
    for k, (w, o_ref) in enumerate(zip(ws, (o1_ref, o2_ref, o3_ref))):
        if k == 0:
            o = o_ref[...].astype(F32)
        else:
            o = jnp.concatenate(natural(
                lambda r, c, o_ref=o_ref: o_ref[:, pl.ds((r * slabs + c) * LANES, LANES)].astype(F32),
                k, slabs, oslab_ref), axis=1)
        term = _dot((w * inv).astype(BF16), ex_ref[...]) * o
        ya = term if ya is None else ya + term
    a = _dot(ya.astype(BF16), wa_ref[...])
    merged = ga_ref[...].astype(F32) * a + gg_ref[...].astype(F32) * b
    return x_ref[...] + _dot(merged.astype(BF16), wo_ref[...])


def _ffn_body(x, p_ref, nf_ref, np_ref, nl_ref, wgu_ref, wd_ref, wpe_ref, wpg_ref, chunk):
    h = _rms(x, nf_ref[...]).astype(BF16)
    hidden = wd_ref.shape[0]
    acc = x
    for c in range(0, hidden, chunk):
        g = _dot(h, wgu_ref[:, c:c + chunk])
        u = _dot(h, wgu_ref[:, hidden + c:hidden + c + chunk])
        acc = acc + _dot((g * jax.nn.sigmoid(g) * u).astype(BF16), wd_ref[c:c + chunk, :])
    hp = _rms(acc, np_ref[...]).astype(BF16)
    pe = _dot(p_ref[...].astype(BF16), wpe_ref[...])
    x3 = acc + pe * jax.nn.sigmoid(_dot(hp, wpg_ref[...]))
    return _rms(x3, nl_ref[...])


def _back_kernel(*refs, dilations, chunk):
    rows, p_ref, consts, norms, ffn_w = refs[:10], refs[10], refs[11:15], refs[15:18], refs[18:22]
    out_ref, oslab_ref, lslab_ref = refs[22:]
    x1 = _merge_body(*rows, *consts, oslab_ref, lslab_ref, dilations)
    out_ref[...] = _ffn_body(x1, p_ref, *norms, *ffn_w, chunk)


def _back_call(x2, os_, lses, yb, ga, gg, p2, wa, wb, wo, nf, npl, nl, wgu, wd, wpe, wpg, *, tm,
               dilations, chunk):
    m, d = x2.shape
    width = yb.shape[1]
    assert wd.shape[0] % chunk == 0 and wgu.shape[1] == 2 * wd.shape[0]
    expand = np.zeros((LANES, width), np.float32)
    for h in range(width // A_HEAD_DIM):
        expand[h, h * A_HEAD_DIM:(h + 1) * A_HEAD_DIM] = 1.0
    expand = jnp.asarray(expand, BF16)
    row = lambda n: pl.BlockSpec((tm, n), lambda i: (i, 0))
    dil = lambda n: [pl.BlockSpec((tm // dl, dl * n), lambda i: (i, 0)) for dl in dilations]
    consts = (expand, wa, wb, wo, nf, npl, nl, wgu, wd, wpe, wpg)
    return pl.pallas_call(
        functools.partial(_back_kernel, dilations=dilations, chunk=chunk),
        grid=(m // tm,),
        in_specs=[row(d)] + dil(width) + dil(LANES) + [row(width), row(d), row(d), row(p2.shape[1])]
        + [_const_spec(c.shape) for c in consts],
        out_specs=row(d),
        out_shape=jax.ShapeDtypeStruct((m, d), F32),
        scratch_shapes=[pltpu.VMEM((2, width // LANES, tm, LANES), F32),
                        pltpu.VMEM((2, 1, tm, LANES), F32)],
        compiler_params=_params("parallel"),
        name="back",
    )(x2, *os_, *lses, yb, ga, gg, p2, *consts)


def kernel(x, p, norm_mix, w_in, hg_lb, hg_norm, w_a_up, w_b_up, w_out, norm_ffn, w_gu, w_down,
           norm_ple, w_pe, w_pg, norm_final):
    batch, seq, d = x.shape
    depth = p.shape[0]
    assert depth == 1 and hg_lb.shape[0] == 2
    m = batch * seq
    tm = 512
    bf = lambda w: w.astype(BF16)
    vec = lambda g: g.reshape(1, -1).astype(F32)

    dilations = tuple(dl for _, dl in A_PATTERNS)
    x2 = x.reshape(m, d)
    outs = _front_call(x2, vec(norm_mix[0]), bf(w_in[0]), hg_lb.astype(F32), vec(hg_norm[0]),
                       batch=batch, seq=seq, tm=tm, dilations=dilations)
    nd = len(dilations)
    qs, ks, vs = outs[:nd], outs[nd:2 * nd], outs[2 * nd:3 * nd]
    yb, ga, gg = outs[3 * nd:]
    os_, lses = [], []
    for g, (window, dilation) in enumerate(A_PATTERNS):
        o, lse = _attn_call(qs[g], ks[g], vs[g], batch=batch, seq=seq, window=window, dilation=dilation)
        os_.append(o)
        lses.append(lse)
    out = _back_call(x2, os_, lses, yb, ga, gg, p[0].reshape(m, -1), bf(w_a_up[0]), bf(w_b_up[0]),
                     bf(w_out[0]), vec(norm_ffn[0]), vec(norm_ple[0]), vec(norm_final), bf(w_gu[0]),
                     bf(w_down[0]), bf(w_pe[0]), bf(w_pg[0]), tm=tm, dilations=dilations, chunk=256)
    return out.reshape(batch, seq, d)
```

```python
import functools

import jax
import jax.numpy as jnp
import numpy as np
from jax import lax
from jax.experimental import pallas as pl
from jax.experimental.pallas import tpu as pltpu

F32 = jnp.float32
BF16 = jnp.bfloat16

LANES = 128
EPS = 1e-6
NEG = -1e30
LOG2E = 1.4426950408889634

A_HEAD_DIM = 64
A_PATTERNS = ((128, 1), (512, 4), (2048, 16))
A_BLK = 128
ATTN_UNITS = 8
B_HEAD_DIM = 128
HG_TILE = 256

VMEM_LIMIT = 56 * 1024 * 1024


def _params(*sem):
    return pltpu.CompilerParams(dimension_semantics=sem, vmem_limit_bytes=VMEM_LIMIT)


def _const_spec(shape):
    nd = len(shape)
    return pl.BlockSpec(shape, lambda *_: (0,) * nd, pipeline_mode=pl.Buffered(1))


def _dot(a, b):
    return jnp.dot(a, b, preferred_element_type=F32)


def _dot_nt(a, b):
    return lax.dot_general(a, b, (((1,), (1,)), ((), ())), preferred_element_type=F32)


def _dot_tn(a, b):
    return lax.dot_general(a, b, (((0,), (0,)), ((), ())), preferred_element_type=F32)


def _rms(x, g):
    return x * lax.rsqrt(jnp.mean(x * x, axis=-1, keepdims=True) + EPS) * g


SUB = 8
HG_BLOCK = 128
HG_SAFE_LOG2 = 100.0


def _rep(t, n):
    return jnp.concatenate([t] * n, axis=0)


def _small_levels(lf2, visit=None):
    sub = lax.broadcasted_iota(jnp.int32, (1, SUB, lf2.shape[-1]), 1)
    pre = tot = lf2.reshape(lf2.shape[0] // SUB, SUB, lf2.shape[-1])
    m, level = 1, 1
    while m < SUB:
        upper = (sub & m) != 0
        if visit is not None:
            visit(level, upper, pre, tot)
        below = pltpu.roll(tot, m, 1)
        above = pltpu.roll(tot, SUB - m, 1)
        pre = pre + jnp.where(upper, below, 0.0)
        tot = tot + jnp.where(upper, below, above)
        m, level = 2 * m, level + 1
    return pre, tot


def _log_decay(lf2):
    sub = lax.broadcasted_iota(jnp.int32, (1, SUB, lf2.shape[-1]), 1)
    pre = lf2.reshape(lf2.shape[0] // SUB, SUB, lf2.shape[-1])
    k = 1
    while k < SUB:
        pre = pre + jnp.where(sub >= k, pltpu.roll(pre, k, 1), 0.0)
        k *= 2
    tot = jnp.broadcast_to(pre[:, SUB - 1:SUB, :], pre.shape)
    ends = [tot[0]]
    for g in range(1, pre.shape[0]):
        ends.append(ends[-1] + tot[g])
    b = jnp.concatenate([pre[0]] + [pre[g] + ends[g - 1] for g in range(1, len(ends))], axis=0)
    return b, ends


def _hgrn_unit(q, kk, lf2, b, ends, v, gact, gn, lvl, state_ref, head, block_start):
    tile = q.shape[0]
    groups, half = tile // SUB, tile // 2
    quads = [slice(r0, r0 + half) for r0 in (0, half)]

    def level_factors(m):
        parts = []
        for lo in range(0, tile, 2 * m):
            mid = _rep(ends[(lo + m) // SUB - 1], m // SUB)
            parts.append(kk[lo:lo + m] * jnp.exp2(mid - b[lo:lo + m]))
            parts.append(q[lo + m:lo + 2 * m] * jnp.exp2(b[lo + m:lo + 2 * m] - mid))
        return jnp.concatenate(parts, axis=0)

    def add_level(level, y):
        yb16 = y.astype(BF16)
        for i, rows in enumerate(quads):
            diag[i] = jnp.where(lvl == level, _dot_nt(yb16[rows], yb16[rows]), diag[i])

    if block_start:
        qs, ks = [], []
        for lo in range(0, tile, HG_BLOCK):
            d = b[lo:lo + HG_BLOCK]
            if lo:
                d = d - _rep(ends[lo // SUB - 1], HG_BLOCK // SUB)
            qs.append(q[lo:lo + HG_BLOCK] * jnp.exp2(d))
            ks.append(kk[lo:lo + HG_BLOCK] * jnp.exp2(-d))
        qf = jnp.concatenate(qs, axis=0).astype(BF16)
        kf = jnp.concatenate(ks, axis=0).astype(BF16)
        diag = [jnp.where(lvl >= 0, _dot_nt(qf[rows], kf[rows]), 0.0) for rows in quads]
        m, level = HG_BLOCK, HG_BLOCK.bit_length()
    else:
        qb16, kb16 = q.astype(BF16), kk.astype(BF16)
        diag = [jnp.where(lvl == 0, _dot_nt(qb16[rows], kb16[rows]), 0.0) for rows in quads]
        g3 = lambda t: t.reshape(groups, SUB, t.shape[-1])
        q3, k3 = g3(q), g3(kk)

        def small(level, upper, pre, tot):
            y3 = jnp.where(upper, q3, k3) * jnp.exp2(jnp.where(upper, pre, tot - pre))
            add_level(level, y3.reshape(tile, -1))

        _small_levels(lf2, small)
        m, level = SUB, SUB.bit_length()
    while m < half:
        add_level(level, level_factors(m))
        m, level = 2 * m, level + 1
    top = level_factors(half).astype(BF16)
    cross = _dot_nt(top[half:], top[:half]).astype(BF16)
    diag = [sc.astype(BF16) for sc in diag]

    st = state_ref[head]
    whole = _rep(ends[-1], groups)
    inter = _dot_nt((q * jnp.exp2(b)).astype(BF16), st.astype(BF16))
    o = inter + jnp.concatenate(
        [_dot(diag[0], v[:half]), _dot(jnp.concatenate([cross, diag[1]], axis=1), v)], axis=0)
    k_end = (kk * jnp.exp2(whole - b)).astype(BF16)
    state_ref[head] = st * jnp.exp2(ends[-1][0:1, :]) + _dot_tn(v, k_end)
    return _rms(o, gn) * gact


def _front_kernel(lvl_ref, hglb_ref, gn_ref, x_ref, g_ref, w_ref, *refs, width, dilations):
    nd = len(dilations)
    attn_refs = refs[:3 * nd]
    yb_ref, ga_ref, gg_ref, hx_ref, slab_ref, state_ref = refs[3 * nd:]
    tm = x_ref.shape[0]
    heads = width // B_HEAD_DIM

    @pl.when(pl.program_id(1) == 0)
    def _():
        state_ref[...] = jnp.zeros_like(state_ref)

    hx_ref[...] = _rms(x_ref[...], g_ref[...]).astype(BF16)
    half = tm // 2
    slabs = width // LANES

    def proj(t0, rows, c0, n):
        return _dot(hx_ref[pl.ds(t0, rows), :], w_ref[:, c0:c0 + n])

    def attn_job(a, t0):
        y = proj(t0, half, a * width, width)
        if a == 0:
            y = y * (A_HEAD_DIM ** -0.5 * LOG2E)
        for c in range(slabs):
            slab_ref[0, a * slabs + c, pl.ds(t0, half), :] = y[:, c * LANES:(c + 1) * LANES]
        d_prev = 1
        for k, (out_ref, d) in enumerate(zip(attn_refs[a * nd:(a + 1) * nd], dilations)):
            if d == 1:
                out_ref[pl.ds(t0, half), :] = y.astype(BF16)
                continue
            ratio, n_prev, n = d // d_prev, half // d_prev, half // d
            for r in range(d):
                r_prev, j = r % d_prev, r // d_prev
                for c in range(slabs):
                    piece = slab_ref[k - 1, a * slabs + c,
                                     pl.ds(t0 + r_prev * n_prev + j, n, stride=ratio), :]
                    out_ref[pl.ds(t0 // d, n), pl.ds((r * slabs + c) * LANES, LANES)] = piece.astype(BF16)
                    if k + 1 < nd:
                        slab_ref[k, a * slabs + c, pl.ds(t0 + r * n, n), :] = piece
            d_prev = d

    def gate_job(out_ref, c0, j, t0):
        y = proj(t0, half, c0 + j * width, width)
        out_ref[pl.ds(t0, half), pl.ds(j * width, width)] = jax.nn.sigmoid(y).astype(BF16)

    jobs = [functools.partial(attn_job, a, t0) for a in range(3) for t0 in (0, half)]
    jobs += [functools.partial(gate_job, ref, c0, j, t0)
             for ref, c0 in ((ga_ref, 7 * width), (gg_ref, 9 * width)) for j in range(2)
             for t0 in (0, half)]

    hg = hglb_ref[...]
    ex = jnp.exp(hg - jnp.max(hg, axis=0, keepdims=True))
    lb_all = ex[0:1] / jnp.sum(ex, axis=0, keepdims=True)
    lvl = lvl_ref[...]

    def head_proj(hd):
        w_head = jnp.concatenate(
            [w_ref[:, pl.ds((3 + j) * width + hd * B_HEAD_DIM, B_HEAD_DIM)] for j in range(4)], axis=1)
        hp = _dot(hx_ref[...], w_head)
        q, fgate, v, gate = (hp[:, j * B_HEAD_DIM:(j + 1) * B_HEAD_DIM] for j in range(4))
        lb = lb_all[:, hd * B_HEAD_DIM:(hd + 1) * B_HEAD_DIM]
        f = lb + (1.0 - lb) * jax.nn.sigmoid(fgate)
        return q, 1.0 - f, jnp.log2(f), v.astype(BF16), gate * jax.nn.sigmoid(gate)

    units, worst = [], None
    for hd in range(heads):
        data = head_proj(hd)
        for t0 in range(0, tm, HG_TILE):
            q, kk, lf2, v, gact = (t[t0:t0 + HG_TILE] for t in data)
            b, ends = _log_decay(lf2)
            for lo in range(0, HG_TILE, HG_BLOCK):
                blk = ends[(lo + HG_BLOCK) // SUB - 1]
                if lo:
                    blk = blk - ends[lo // SUB - 1]
                worst = blk if worst is None else jnp.minimum(worst, blk)
            units.append((hd, t0, q, kk, lf2, b, ends, v, gact))
    for job in jobs:
        job()

    def run(block_start):
        for hd, t0, *args in units:
            y = _hgrn_unit(*args, gn_ref[...], lvl, state_ref, hd, block_start)
            yb_ref[pl.ds(t0, HG_TILE), pl.ds(hd * B_HEAD_DIM, B_HEAD_DIM)] = y.astype(yb_ref.dtype)

    safe = jnp.min(worst) >= -HG_SAFE_LOG2
    pl.when(safe)(functools.partial(run, True))
    pl.when(jnp.logical_not(safe))(functools.partial(run, False))


def _level_map(tile):
    t = np.arange(tile)[:, None]
    s = np.arange(tile)[None, :]
    x = np.maximum(t ^ s, 1)
    lv = np.floor(np.log2(x)).astype(np.int32) + 1
    return jnp.asarray(np.where(s < t, lv, np.where(s == t, 0, -1)), jnp.int32)


def _front_call(x2, g, w, hg_lb, g_norm, *, batch, seq, tm, dilations):
    m, d = x2.shape
    width = d // 2
    assert seq % tm == 0 and tm % HG_TILE == 0
    assert dilations[0] == 1 and all(b % a == 0 for a, b in zip(dilations, dilations[1:]))
    steps = seq // tm
    row = lambda n: pl.BlockSpec((tm, n), lambda b, t: (b * steps + t, 0))
    sds = lambda n, dt: jax.ShapeDtypeStruct((m, n), dt)
    dil_specs = [pl.BlockSpec((tm // dl, dl * width), lambda b, t: (b * steps + t, 0))
                 for dl in dilations] * 3
    dil_shapes = [jax.ShapeDtypeStruct((m // dl, dl * width), BF16) for dl in dilations] * 3
    return pl.pallas_call(
        functools.partial(_front_kernel, width=width, dilations=dilations),
        grid=(batch, steps),
        in_specs=[_const_spec((HG_TILE // 2, HG_TILE // 2)), _const_spec(hg_lb.shape),
                  _const_spec((1, B_HEAD_DIM)), row(d), _const_spec((1, d)), _const_spec(w.shape)],
        out_specs=dil_specs + [row(width), row(d), row(d)],
        out_shape=dil_shapes + [sds(width, BF16), sds(d, BF16), sds(d, BF16)],
        scratch_shapes=[pltpu.VMEM((tm, d), BF16),
                        pltpu.VMEM((len(dilations) - 1, 3 * width // LANES, tm, LANES), F32),
                        pltpu.VMEM((width // B_HEAD_DIM, B_HEAD_DIM, B_HEAD_DIM), F32)],
        compiler_params=_params("parallel", "arbitrary"),
        name="front",
    )(_level_map(HG_TILE // 2), hg_lb, g_norm, x2, g, w)


def _attn_kernel(bias_ref, q_ref, k_ref, v_ref, o_ref, lse_ref, *, tq, rblk):
    width = q_ref.shape[2] // rblk
    pairs = width // LANES
    lane = lax.broadcasted_iota(jnp.int32, (A_BLK, LANES), 1)
    low_half = lane < A_HEAD_DIM
    ones = jnp.ones((2 * A_BLK, LANES), BF16)
    for rr in range(rblk):
        for qi in range(tq // A_BLK):
            n = pl.program_id(2) * (tq // A_BLK) + qi
            start = pl.multiple_of(jnp.maximum(n - 1, 0) * A_BLK, A_BLK)
            bias = bias_ref[jnp.where(n == 0, 1, 0)]
            rows = pl.ds(qi * A_BLK, A_BLK)
            mx_tile = jnp.zeros((A_BLK, LANES), F32)
            den_tile = jnp.ones((A_BLK, LANES), F32)
            for p in range(pairs):
                cols = pl.ds(rr * width + p * LANES, LANES)
                q_pair = q_ref[0, rows, cols]
                k_band = k_ref[0, pl.ds(start, 2 * A_BLK), cols]
                v_cat = jnp.concatenate([v_ref[0, pl.ds(start, 2 * A_BLK), cols], ones], axis=1)
                nums, dens = [], []
                for hh in range(2):
                    mine = low_half if hh == 0 else jnp.logical_not(low_half)
                    s = _dot_nt(jnp.where(mine, q_pair, jnp.zeros_like(q_pair)), k_band) + bias
                    mx = jnp.max(s, axis=-1, keepdims=True)
                    r = _dot(jnp.exp2(s - mx).astype(BF16), v_cat)
                    nums.append(r[:, :LANES])
                    dens.append(r[:, LANES:])
                    mx_tile = jnp.where(lane == 2 * p + hh, mx, mx_tile)
                    den_tile = jnp.where(lane == 2 * p + hh, dens[-1], den_tile)
                o_ref[0, rows, cols] = (jnp.where(low_half, nums[0], nums[1])
                                        / jnp.where(low_half, dens[0], dens[1])).astype(o_ref.dtype)
            lse_ref[0, rows, pl.ds(rr * LANES, LANES)] = mx_tile + jnp.log2(den_tile)


def _band_bias(span):
    a = np.arange(A_BLK)[:, None]
    c = np.arange(2 * A_BLK)[None, :]
    rel = a + A_BLK - c
    normal = (rel >= 0) & (rel <= span)
    rel0 = a - c
    first = (rel0 >= 0) & (rel0 <= span)
    return jnp.asarray(np.where(np.stack([normal, first]), 0.0, NEG), F32)


def _attn_call(q, k, v, *, batch, seq, window, dilation):
    width = q.shape[1] // dilation
    assert seq % (dilation * A_BLK) == 0 and window % dilation == 0
    length = seq // dilation
    tq = min(length, ATTN_UNITS * A_BLK)
    rblk = min(dilation, ATTN_UNITS * A_BLK // tq)
    sub = lambda t: t.reshape(batch, length, dilation * width)
    o, lse = pl.pallas_call(
        functools.partial(_attn_kernel, tq=tq, rblk=rblk),
        grid=(batch, dilation // rblk, length // tq),
        in_specs=[
            _const_spec((2, A_BLK, 2 * A_BLK)),
            pl.BlockSpec((1, tq, rblk * width), lambda b, r, t: (b, t, r)),
            pl.BlockSpec((1, length, rblk * width), lambda b, r, t: (b, 0, r)),
            pl.BlockSpec((1, length, rblk * width), lambda b, r, t: (b, 0, r)),
        ],
        out_specs=[
            pl.BlockSpec((1, tq, rblk * width), lambda b, r, t: (b, t, r)),
            pl.BlockSpec((1, tq, rblk * LANES), lambda b, r, t: (b, t, r)),
        ],
        out_shape=[
            jax.ShapeDtypeStruct((batch, length, dilation * width), BF16),
            jax.ShapeDtypeStruct((batch, length, dilation * LANES), F32),
        ],
        compiler_params=_params("parallel", "parallel", "arbitrary"),
        name=f"attn_d{dilation}",
    )(_band_bias(window // dilation), sub(q), sub(k), sub(v))
    return (o.reshape(batch * length, dilation * width), lse.reshape(batch * length, dilation * LANES))


def _merge_body(x_ref, o1_ref, o2_ref, o3_ref, l1_ref, l2_ref, l3_ref, yb_ref, ga_ref, gg_ref,
                ex_ref, wa_ref, wb_ref, wo_ref, oslab_ref, lslab_ref, dilations):
    tm = x_ref.shape[0]
    slabs = oslab_ref.shape[1]
    b = _dot(yb_ref[...], wb_ref[...])

    def natural(piece, k, nslab, bufs):
        src = piece
        for level in range(k, 0, -1):
            d, d_prev = dilations[level], dilations[level - 1]
            ratio, n, n_prev = d // d_prev, tm // d, tm // d_prev
            dst = bufs.at[(level - 1) % 2]
            for r in range(d):
                r_prev, j = r % d_prev, r // d_prev
                for c in range(nslab):
                    dst[c, pl.ds(r_prev * n_prev + j, n, stride=ratio), :] = src(r, c)
            src = lambda r, c, dst=dst, n_prev=n_prev: dst[c, pl.ds(r * n_prev, n_prev), :]
        return [bufs[0, c] for c in range(nslab)]

    lses = []
    for k, l_ref in enumerate((l1_ref, l2_ref, l3_ref)):
        if k == 0:
            lses.append(l_ref[...])
        else:
            lses.append(natural(lambda r, c, l_ref=l_ref: l_ref[:, pl.ds(r * LANES, LANES)],
                                k, 1, lslab_ref)[0])
    top = jnp.maximum(jnp.maximum(lses[0], lses[1]), lses[2])
    ws = [jnp.exp2(l - top) for l in lses]
    inv = 1.0 / (ws[0] + ws[1] + ws[2])
    ya = None
    for k, (w, o_ref) in enumerate(zip(ws, (o1_ref, o2_ref, o3_ref))):
        if k == 0:
            o = o_ref[...].astype(F32)
        else:
            o = jnp.concatenate(natural(
                lambda r, c, o_ref=o_ref: o_ref[:, pl.ds((r * slabs + c) * LANES, LANES)].astype(F32),
                k, slabs, oslab_ref), axis=1)
        term = _dot((w * inv).astype(BF16), ex_ref[...]) * o
        ya = term if ya is None else ya + term
    a = _dot(ya.astype(BF16), wa_ref[...])
    merged = ga_ref[...].astype(F32) * a + gg_ref[...].astype(F32) * b
    return x_ref[...] + _dot(merged.astype(BF16), wo_ref[...])


def _ffn_body(x, p_ref, nf_ref, np_ref, nl_ref, wgu_ref, wd_ref, wpe_ref, wpg_ref, chunk):
    h = _rms(x, nf_ref[...]).astype(BF16)
    hidden = wd_ref.shape[0]
    acc = x
    for c in range(0, hidden, chunk):
        g = _dot(h, wgu_ref[:, c:c + chunk])
        u = _dot(h, wgu_ref[:, hidden + c:hidden + c + chunk])
        acc = acc + _dot((g * jax.nn.sigmoid(g) * u).astype(BF16), wd_ref[c:c + chunk, :])
    hp = _rms(acc, np_ref[...]).astype(BF16)
    pe = _dot(p_ref[...].astype(BF16), wpe_ref[...])
    x3 = acc + pe * jax.nn.sigmoid(_dot(hp, wpg_ref[...]))
    return _rms(x3, nl_ref[...])


def _back_kernel(*refs, dilations, chunk):
    rows, p_ref, consts, norms, ffn_w = refs[:10], refs[10], refs[11:15], refs[15:18], refs[18:22]
    out_ref, oslab_ref, lslab_ref = refs[22:]
    x1 = _merge_body(*rows, *consts, oslab_ref, lslab_ref, dilations)
    out_ref[...] = _ffn_body(x1, p_ref, *norms, *ffn_w, chunk)


def _back_call(x2, os_, lses, yb, ga, gg, p2, wa, wb, wo, nf, npl, nl, wgu, wd, wpe, wpg, *, tm,
               dilations, chunk):
    m, d = x2.shape
    width = yb.shape[1]
    assert wd.shape[0] % chunk == 0 and wgu.shape[1] == 2 * wd.shape[0]
    expand = np.zeros((LANES, width), np.float32)
    for h in range(width // A_HEAD_DIM):
        expand[h, h * A_HEAD_DIM:(h + 1) * A_HEAD_DIM] = 1.0
    expand = jnp.asarray(expand, BF16)
    row = lambda n: pl.BlockSpec((tm, n), lambda i: (i, 0))
    dil = lambda n: [pl.BlockSpec((tm // dl, dl * n), lambda i: (i, 0)) for dl in dilations]
    consts = (expand, wa, wb, wo, nf, npl, nl, wgu, wd, wpe, wpg)
    return pl.pallas_call(
        functools.partial(_back_kernel, dilations=dilations, chunk=chunk),
        grid=(m // tm,),
        in_specs=[row(d)] + dil(width) + dil(LANES) + [row(width), row(d), row(d), row(p2.shape[1])]
        + [_const_spec(c.shape) for c in consts],
        out_specs=row(d),
        out_shape=jax.ShapeDtypeStruct((m, d), F32),
        scratch_shapes=[pltpu.VMEM((2, width // LANES, tm, LANES), F32),
                        pltpu.VMEM((2, 1, tm, LANES), F32)],
        compiler_params=_params("parallel"),
        name="back",
    )(x2, *os_, *lses, yb, ga, gg, p2, *consts)


def kernel(x, p, norm_mix, w_in, hg_lb, hg_norm, w_a_up, w_b_up, w_out, norm_ffn, w_gu, w_down,
           norm_ple, w_pe, w_pg, norm_final):
    batch, seq, d = x.shape
    depth = p.shape[0]
    assert depth == 1 and hg_lb.shape[0] == 2
    m = batch * seq
    tm = 512
    bf = lambda w: w.astype(BF16)
    vec = lambda g: g.reshape(1, -1).astype(F32)

    dilations = tuple(dl for _, dl in A_PATTERNS)
    x2 = x.reshape(m, d)
    outs = _front_call(x2, vec(norm_mix[0]), bf(w_in[0]), hg_lb.astype(F32), vec(hg_norm[0]),
                       batch=batch, seq=seq, tm=tm, dilations=dilations)
    nd = len(dilations)
    qs, ks, vs = outs[:nd], outs[nd:2 * nd], outs[2 * nd:3 * nd]
    yb, ga, gg = outs[3 * nd:]
    os_, lses = [], []
    for g, (window, dilation) in enumerate(A_PATTERNS):
        o, lse = _attn_call(qs[g], ks[g], vs[g], batch=batch, seq=seq, window=window, dilation=dilation)
        os_.append(o)
        lses.append(lse)
    out = _back_call(x2, os_, lses, yb, ga, gg, p[0].reshape(m, -1), bf(w_a_up[0]), bf(w_b_up[0]),
                     bf(w_out[0]), vec(norm_ffn[0]), vec(norm_ple[0]), vec(norm_final), bf(w_gu[0]),
                     bf(w_down[0]), bf(w_pe[0]), bf(w_pg[0]), tm=tm, dilations=dilations, chunk=256)
    return out.reshape(batch, seq, d)
```

```python
import functools

import jax
import jax.numpy as jnp
import numpy as np
from jax import lax
from jax.experimental import pallas as pl
from jax.experimental.pallas import tpu as pltpu

F32 = jnp.float32
BF16 = jnp.bfloat16

LANES = 128
EPS = 1e-6
NEG = -1e30
LOG2E = 1.4426950408889634

A_HEAD_DIM = 64
A_PATTERNS = ((128, 1), (512, 4), (2048, 16))
A_BLK = 128
ATTN_UNITS = 16
B_HEAD_DIM = 128
HG_TILE = 256

VMEM_LIMIT = 56 * 1024 * 1024


def _params(*sem):
    return pltpu.CompilerParams(dimension_semantics=sem, vmem_limit_bytes=VMEM_LIMIT)


def _const_spec(shape):
    nd = len(shape)
    return pl.BlockSpec(shape, lambda *_: (0,) * nd, pipeline_mode=pl.Buffered(1))


def _dot(a, b):
    return jnp.dot(a, b, preferred_element_type=F32)


def _dot_nt(a, b):
    return lax.dot_general(a, b, (((1,), (1,)), ((), ())), preferred_element_type=F32)


def _dot_tn(a, b):
    return lax.dot_general(a, b, (((0,), (0,)), ((), ())), preferred_element_type=F32)


def _rms(x, g):
    return x * lax.rsqrt(jnp.mean(x * x, axis=-1, keepdims=True) + EPS) * g


SUB = 8
HG_BLOCK = 128
HG_SAFE_LOG2 = 100.0


def _rep(t, n):
    return jnp.concatenate([t] * n, axis=0)


def _small_levels(lf2, visit=None):
    sub = lax.broadcasted_iota(jnp.int32, (1, SUB, lf2.shape[-1]), 1)
    pre = tot = lf2.reshape(lf2.shape[0] // SUB, SUB, lf2.shape[-1])
    m, level = 1, 1
    while m < SUB:
        upper = (sub & m) != 0
        if visit is not None:
            visit(level, upper, pre, tot)
        below = pltpu.roll(tot, m, 1)
        above = pltpu.roll(tot, SUB - m, 1)
        pre = pre + jnp.where(upper, below, 0.0)
        tot = tot + jnp.where(upper, below, above)
        m, level = 2 * m, level + 1
    return pre, tot


def _log_decay(lf2):
    sub = lax.broadcasted_iota(jnp.int32, (1, SUB, lf2.shape[-1]), 1)
    pre = lf2.reshape(lf2.shape[0] // SUB, SUB, lf2.shape[-1])
    k = 1
    while k < SUB:
        pre = pre + jnp.where(sub >= k, pltpu.roll(pre, k, 1), 0.0)
        k *= 2
    tot = jnp.broadcast_to(pre[:, SUB - 1:SUB, :], pre.shape)
    ends = [tot[0]]
    for g in range(1, pre.shape[0]):
        ends.append(ends[-1] + tot[g])
    b = jnp.concatenate([pre[0]] + [pre[g] + ends[g - 1] for g in range(1, len(ends))], axis=0)
    return b, ends


def _hgrn_unit(q, kk, lf2, b, ends, v, gact, gn, lvl, state_ref, head, block_start):
    tile = q.shape[0]
    groups, half = tile // SUB, tile // 2
    quads = [slice(r0, r0 + half) for r0 in (0, half)]

    def level_factors(m):
        parts = []
        for lo in range(0, tile, 2 * m):
            mid = _rep(ends[(lo + m) // SUB - 1], m // SUB)
            parts.append(kk[lo:lo + m] * jnp.exp2(mid - b[lo:lo + m]))
            parts.append(q[lo + m:lo + 2 * m] * jnp.exp2(b[lo + m:lo + 2 * m] - mid))
        return jnp.concatenate(parts, axis=0)

    def add_level(level, y):
        yb16 = y.astype(BF16)
        for i, rows in enumerate(quads):
            diag[i] = jnp.where(lvl == level, _dot_nt(yb16[rows], yb16[rows]), diag[i])

    if block_start:
        qs, ks = [], []
        for lo in range(0, tile, HG_BLOCK):
            d = b[lo:lo + HG_BLOCK]
            if lo:
                d = d - _rep(ends[lo // SUB - 1], HG_BLOCK // SUB)
            qs.append(q[lo:lo + HG_BLOCK] * jnp.exp2(d))
            ks.append(kk[lo:lo + HG_BLOCK] * jnp.exp2(-d))
        qf = jnp.concatenate(qs, axis=0).astype(BF16)
        kf = jnp.concatenate(ks, axis=0).astype(BF16)
        diag = [jnp.where(lvl >= 0, _dot_nt(qf[rows], kf[rows]), 0.0) for rows in quads]
        m, level = HG_BLOCK, HG_BLOCK.bit_length()
    else:
        qb16, kb16 = q.astype(BF16), kk.astype(BF16)
        diag = [jnp.where(lvl == 0, _dot_nt(qb16[rows], kb16[rows]), 0.0) for rows in quads]
        g3 = lambda t: t.reshape(groups, SUB, t.shape[-1])
        q3, k3 = g3(q), g3(kk)

        def small(level, upper, pre, tot):
            y3 = jnp.where(upper, q3, k3) * jnp.exp2(jnp.where(upper, pre, tot - pre))
            add_level(level, y3.reshape(tile, -1))

        _small_levels(lf2, small)
        m, level = SUB, SUB.bit_length()
    while m < half:
        add_level(level, level_factors(m))
        m, level = 2 * m, level + 1
    top = level_factors(half).astype(BF16)
    cross = _dot_nt(top[half:], top[:half]).astype(BF16)
    diag = [sc.astype(BF16) for sc in diag]

    st = state_ref[head]
    whole = _rep(ends[-1], groups)
    inter = _dot_nt((q * jnp.exp2(b)).astype(BF16), st.astype(BF16))
    o = inter + jnp.concatenate(
        [_dot(diag[0], v[:half]), _dot(jnp.concatenate([cross, diag[1]], axis=1), v)], axis=0)
    k_end = (kk * jnp.exp2(whole - b)).astype(BF16)
    state_ref[head] = st * jnp.exp2(ends[-1][0:1, :]) + _dot_tn(v, k_end)
    return _rms(o, gn) * gact


def _front_kernel(lvl_ref, hglb_ref, gn_ref, x_ref, g_ref, w_ref, *refs, width, dilations):
    nd = len(dilations)
    attn_refs = refs[:3 * nd]
    yb_ref, ga_ref, gg_ref, hx_ref, slab_ref, state_ref = refs[3 * nd:]
    tm = x_ref.shape[0]
    heads = width // B_HEAD_DIM

    @pl.when(pl.program_id(1) == 0)
    def _():
        state_ref[...] = jnp.zeros_like(state_ref)

    hx_ref[...] = _rms(x_ref[...], g_ref[...]).astype(BF16)
    half = tm // 2
    slabs = width // LANES

    def proj(t0, rows, c0, n):
        return _dot(hx_ref[pl.ds(t0, rows), :], w_ref[:, c0:c0 + n])

    def attn_job(a, t0):
        y = proj(t0, half, a * width, width)
        if a == 0:
            y = y * (A_HEAD_DIM ** -0.5 * LOG2E)
        for c in range(slabs):
            slab_ref[0, a * slabs + c, pl.ds(t0, half), :] = y[:, c * LANES:(c + 1) * LANES]
        d_prev = 1
        for k, (out_ref, d) in enumerate(zip(attn_refs[a * nd:(a + 1) * nd], dilations)):
            if d == 1:
                out_ref[pl.ds(t0, half), :] = y.astype(BF16)
                continue
            ratio, n_prev, n = d // d_prev, half // d_prev, half // d
            for r in range(d):
                r_prev, j = r % d_prev, r // d_prev
                for c in range(slabs):
                    piece = slab_ref[k - 1, a * slabs + c,
                                     pl.ds(t0 + r_prev * n_prev + j, n, stride=ratio), :]
                    out_ref[pl.ds(t0 // d, n), pl.ds((r * slabs + c) * LANES, LANES)] = piece.astype(BF16)
                    if k + 1 < nd:
                        slab_ref[k, a * slabs + c, pl.ds(t0 + r * n, n), :] = piece
            d_prev = d

    def gate_job(out_ref, c0, j, t0):
        y = proj(t0, half, c0 + j * width, width)
        out_ref[pl.ds(t0, half), pl.ds(j * width, width)] = jax.nn.sigmoid(y).astype(BF16)

    jobs = [functools.partial(attn_job, a, t0) for a in range(3) for t0 in (0, half)]
    jobs += [functools.partial(gate_job, ref, c0, j, t0)
             for ref, c0 in ((ga_ref, 7 * width), (gg_ref, 9 * width)) for j in range(2)
             for t0 in (0, half)]

    hg = hglb_ref[...]
    ex = jnp.exp(hg - jnp.max(hg, axis=0, keepdims=True))
    lb_all = ex[0:1] / jnp.sum(ex, axis=0, keepdims=True)
    lvl = lvl_ref[...]

    def head_proj(hd):
        w_head = jnp.concatenate(
            [w_ref[:, pl.ds((3 + j) * width + hd * B_HEAD_DIM, B_HEAD_DIM)] for j in range(4)], axis=1)
        hp = _dot(hx_ref[...], w_head)
        q, fgate, v, gate = (hp[:, j * B_HEAD_DIM:(j + 1) * B_HEAD_DIM] for j in range(4))
        lb = lb_all[:, hd * B_HEAD_DIM:(hd + 1) * B_HEAD_DIM]
        f = lb + (1.0 - lb) * jax.nn.sigmoid(fgate)
        return q, 1.0 - f, jnp.log2(f), v.astype(BF16), gate * jax.nn.sigmoid(gate)

    units, worst = [], None
    for hd in range(heads):
        data = head_proj(hd)
        for t0 in range(0, tm, HG_TILE):
            q, kk, lf2, v, gact = (t[t0:t0 + HG_TILE] for t in data)
            b, ends = _log_decay(lf2)
            for lo in range(0, HG_TILE, HG_BLOCK):
                blk = ends[(lo + HG_BLOCK) // SUB - 1]
                if lo:
                    blk = blk - ends[lo // SUB - 1]
                worst = blk if worst is None else jnp.minimum(worst, blk)
            units.append((hd, t0, q, kk, lf2, b, ends, v, gact))
    for job in jobs:
        job()

    def run(block_start):
        for hd, t0, *args in units:
            y = _hgrn_unit(*args, gn_ref[...], lvl, state_ref, hd, block_start)
            yb_ref[pl.ds(t0, HG_TILE), pl.ds(hd * B_HEAD_DIM, B_HEAD_DIM)] = y.astype(yb_ref.dtype)

    safe = jnp.min(worst) >= -HG_SAFE_LOG2
    pl.when(safe)(functools.partial(run, True))
    pl.when(jnp.logical_not(safe))(functools.partial(run, False))


def _level_map(tile):
    t = np.arange(tile)[:, None]
    s = np.arange(tile)[None, :]
    x = np.maximum(t ^ s, 1)
    lv = np.floor(np.log2(x)).astype(np.int32) + 1
    return jnp.asarray(np.where(s < t, lv, np.where(s == t, 0, -1)), jnp.int32)


def _front_call(x2, g, w, hg_lb, g_norm, *, batch, seq, tm, dilations):
    m, d = x2.shape
    width = d // 2
    assert seq % tm == 0 and tm % HG_TILE == 0
    assert dilations[0] == 1 and all(b % a == 0 for a, b in zip(dilations, dilations[1:]))
    steps = seq // tm
    row = lambda n: pl.BlockSpec((tm, n), lambda b, t: (b * steps + t, 0))
    sds = lambda n, dt: jax.ShapeDtypeStruct((m, n), dt)
    dil_specs = [pl.BlockSpec((tm // dl, dl * width), lambda b, t: (b * steps + t, 0))
                 for dl in dilations] * 3
    dil_shapes = [jax.ShapeDtypeStruct((m // dl, dl * width), BF16) for dl in dilations] * 3
    return pl.pallas_call(
        functools.partial(_front_kernel, width=width, dilations=dilations),
        grid=(batch, steps),
        in_specs=[_const_spec((HG_TILE // 2, HG_TILE // 2)), _const_spec(hg_lb.shape),
                  _const_spec((1, B_HEAD_DIM)), row(d), _const_spec((1, d)), _const_spec(w.shape)],
        out_specs=dil_specs + [row(width), row(d), row(d)],
        out_shape=dil_shapes + [sds(width, BF16), sds(d, BF16), sds(d, BF16)],
        scratch_shapes=[pltpu.VMEM((tm, d), BF16),
                        pltpu.VMEM((len(dilations) - 1, 3 * width // LANES, tm, LANES), F32),
                        pltpu.VMEM((width // B_HEAD_DIM, B_HEAD_DIM, B_HEAD_DIM), F32)],
        compiler_params=_params("parallel", "arbitrary"),
        name="front",
    )(_level_map(HG_TILE // 2), hg_lb, g_norm, x2, g, w)


def _attn_kernel(bias_ref, q_ref, k_ref, v_ref, o_ref, lse_ref, *, tq, rblk):
    width = q_ref.shape[2] // rblk
    pairs = width // LANES
    lane = lax.broadcasted_iota(jnp.int32, (A_BLK, LANES), 1)
    low_half = lane < A_HEAD_DIM
    ones = jnp.ones((2 * A_BLK, LANES), BF16)
    for rr in range(rblk):
        for qi in range(tq // A_BLK):
            n = pl.program_id(2) * (tq // A_BLK) + qi
            start = pl.multiple_of(jnp.maximum(n - 1, 0) * A_BLK, A_BLK)
            bias = bias_ref[jnp.where(n == 0, 1, 0)]
            rows = pl.ds(qi * A_BLK, A_BLK)
            mx_tile = jnp.zeros((A_BLK, LANES), F32)
            den_tile = jnp.ones((A_BLK, LANES), F32)
            for p in range(pairs):
                cols = pl.ds(rr * width + p * LANES, LANES)
                q_pair = q_ref[0, rows, cols]
                k_band = k_ref[0, pl.ds(start, 2 * A_BLK), cols]
                v_cat = jnp.concatenate([v_ref[0, pl.ds(start, 2 * A_BLK), cols], ones], axis=1)
                nums, dens = [], []
                for hh in range(2):
                    mine = low_half if hh == 0 else jnp.logical_not(low_half)
                    s = _dot_nt(jnp.where(mine, q_pair, jnp.zeros_like(q_pair)), k_band) + bias
                    mx = jnp.max(s, axis=-1, keepdims=True)
                    r = _dot(jnp.exp2(s - mx).astype(BF16), v_cat)
                    nums.append(r[:, :LANES])
                    dens.append(r[:, LANES:])
                    mx_tile = jnp.where(lane == 2 * p + hh, mx, mx_tile)
                    den_tile = jnp.where(lane == 2 * p + hh, dens[-1], den_tile)
                o_ref[0, rows, cols] = (jnp.where(low_half, nums[0], nums[1])
                                        / jnp.where(low_half, dens[0], dens[1])).astype(o_ref.dtype)
            lse_ref[0, rows, pl.ds(rr * LANES, LANES)] = mx_tile + jnp.log2(den_tile)


def _band_bias(span):
    a = np.arange(A_BLK)[:, None]
    c = np.arange(2 * A_BLK)[None, :]
    rel = a + A_BLK - c
    normal = (rel >= 0) & (rel <= span)
    rel0 = a - c
    first = (rel0 >= 0) & (rel0 <= span)
    return jnp.asarray(np.where(np.stack([normal, first]), 0.0, NEG), F32)


def _attn_call(q, k, v, *, batch, seq, window, dilation):
    width = q.shape[1] // dilation
    assert seq % (dilation * A_BLK) == 0 and window % dilation == 0
    length = seq // dilation
    tq = min(length, ATTN_UNITS * A_BLK)
    rblk = min(dilation, ATTN_UNITS * A_BLK // tq)
    sub = lambda t: t.reshape(batch, length, dilation * width)
    o, lse = pl.pallas_call(
        functools.partial(_attn_kernel, tq=tq, rblk=rblk),
        grid=(batch, dilation // rblk, length // tq),
        in_specs=[
            _const_spec((2, A_BLK, 2 * A_BLK)),
            pl.BlockSpec((1, tq, rblk * width), lambda b, r, t: (b, t, r)),
            pl.BlockSpec((1, length, rblk * width), lambda b, r, t: (b, 0, r)),
            pl.BlockSpec((1, length, rblk * width), lambda b, r, t: (b, 0, r)),
        ],
        out_specs=[
            pl.BlockSpec((1, tq, rblk * width), lambda b, r, t: (b, t, r)),
            pl.BlockSpec((1, tq, rblk * LANES), lambda b, r, t: (b, t, r)),
        ],
        out_shape=[
            jax.ShapeDtypeStruct((batch, length, dilation * width), BF16),
            jax.ShapeDtypeStruct((batch, length, dilation * LANES), F32),
        ],
        compiler_params=_params("parallel", "parallel", "arbitrary"),
        name=f"attn_d{dilation}",
    )(_band_bias(window // dilation), sub(q), sub(k), sub(v))
    return (o.reshape(batch * length, dilation * width), lse.reshape(batch * length, dilation * LANES))


def _merge_body(x_ref, o1_ref, o2_ref, o3_ref, l1_ref, l2_ref, l3_ref, yb_ref, ga_ref, gg_ref,
                ex_ref, wa_ref, wb_ref, wo_ref, oslab_ref, lslab_ref, dilations):
    tm = x_ref.shape[0]
    slabs = oslab_ref.shape[1]
    b = _dot(yb_ref[...], wb_ref[...])

    def natural(piece, k, nslab, bufs):
        src = piece
        for level in range(k, 0, -1):
            d, d_prev = dilations[level], dilations[level - 1]
            ratio, n, n_prev = d // d_prev, tm // d, tm // d_prev
            dst = bufs.at[(level - 1) % 2]
            for r in range(d):
                r_prev, j = r % d_prev, r // d_prev
                for c in range(nslab):
                    dst[c, pl.ds(r_prev * n_prev + j, n, stride=ratio), :] = src(r, c)
            src = lambda r, c, dst=dst, n_prev=n_prev: dst[c, pl.ds(r * n_prev, n_prev), :]
        return [bufs[0, c] for c in range(nslab)]

    lses = []
    for k, l_ref in enumerate((l1_ref, l2_ref, l3_ref)):
        if k == 0:
            lses.append(l_ref[...])
        else:
            lses.append(natural(lambda r, c, l_ref=l_ref: l_ref[:, pl.ds(r * LANES, LANES)],
                                k, 1, lslab_ref)[0])
    top = jnp.maximum(jnp.maximum(lses[0], lses[1]), lses[2])
    ws = [jnp.exp2(l - top) for l in lses]
    inv = 1.0 / (ws[0] + ws[1] + ws[2])
    ya = None
    for k, (w, o_ref) in enumerate(zip(ws, (o1_ref, o2_ref, o3_ref))):
        if k == 0:
            o = o_ref[...].astype(F32)
        else:
            o = jnp.concatenate(natural(
                lambda r, c, o_ref=o_ref: o_ref[:, pl.ds((r * slabs + c) * LANES, LANES)].astype(F32),
                k, slabs, oslab_ref), axis=1)
        term = _dot((w * inv).astype(BF16), ex_ref[...]) * o
        ya = term if ya is None else ya + term
    a = _dot(ya.astype(BF16), wa_ref[...])
    merged = ga_ref[...].astype(F32) * a + gg_ref[...].astype(F32) * b
    return x_ref[...] + _dot(merged.astype(BF16), wo_ref[...])


def _ffn_body(x, p_ref, nf_ref, np_ref, nl_ref, wgu_ref, wd_ref, wpe_ref, wpg_ref, chunk):
    h = _rms(x, nf_ref[...]).astype(BF16)
    hidden = wd_ref.shape[0]
    acc = x
    for c in range(0, hidden, chunk):
        g = _dot(h, wgu_ref[:, c:c + chunk])
        u = _dot(h, wgu_ref[:, hidden + c:hidden + c + chunk])
        acc = acc + _dot((g * jax.nn.sigmoid(g) * u).astype(BF16), wd_ref[c:c + chunk, :])
    hp = _rms(acc, np_ref[...]).astype(BF16)
    pe = _dot(p_ref[...].astype(BF16), wpe_ref[...])
    x3 = acc + pe * jax.nn.sigmoid(_dot(hp, wpg_ref[...]))
    return _rms(x3, nl_ref[...])


def _back_kernel(*refs, dilations, chunk):
    rows, p_ref, consts, norms, ffn_w = refs[:10], refs[10], refs[11:15], refs[15:18], refs[18:22]
    out_ref, oslab_ref, lslab_ref = refs[22:]
    x1 = _merge_body(*rows, *consts, oslab_ref, lslab_ref, dilations)
    out_ref[...] = _ffn_body(x1, p_ref, *norms, *ffn_w, chunk)


def _back_call(x2, os_, lses, yb, ga, gg, p2, wa, wb, wo, nf, npl, nl, wgu, wd, wpe, wpg, *, tm,
               dilations, chunk):
    m, d = x2.shape
    width = yb.shape[1]
    assert wd.shape[0] % chunk == 0 and wgu.shape[1] == 2 * wd.shape[0]
    expand = np.zeros((LANES, width), np.float32)
    for h in range(width // A_HEAD_DIM):
        expand[h, h * A_HEAD_DIM:(h + 1) * A_HEAD_DIM] = 1.0
    expand = jnp.asarray(expand, BF16)
    row = lambda n: pl.BlockSpec((tm, n), lambda i: (i, 0))
    dil = lambda n: [pl.BlockSpec((tm // dl, dl * n), lambda i: (i, 0)) for dl in dilations]
    consts = (expand, wa, wb, wo, nf, npl, nl, wgu, wd, wpe, wpg)
    return pl.pallas_call(
        functools.partial(_back_kernel, dilations=dilations, chunk=chunk),
        grid=(m // tm,),
        in_specs=[row(d)] + dil(width) + dil(LANES) + [row(width), row(d), row(d), row(p2.shape[1])]
        + [_const_spec(c.shape) for c in consts],
        out_specs=row(d),
        out_shape=jax.ShapeDtypeStruct((m, d), F32),
        scratch_shapes=[pltpu.VMEM((2, width // LANES, tm, LANES), F32),
                        pltpu.VMEM((2, 1, tm, LANES), F32)],
        compiler_params=_params("parallel"),
        name="back",
    )(x2, *os_, *lses, yb, ga, gg, p2, *consts)


def kernel(x, p, norm_mix, w_in, hg_lb, hg_norm, w_a_up, w_b_up, w_out, norm_ffn, w_gu, w_down,
           norm_ple, w_pe, w_pg, norm_final):
    batch, seq, d = x.shape
    depth = p.shape[0]
    assert depth == 1 and hg_lb.shape[0] == 2
    m = batch * seq
    tm = 512
    bf = lambda w: w.astype(BF16)
    vec = lambda g: g.reshape(1, -1).astype(F32)

    dilations = tuple(dl for _, dl in A_PATTERNS)
    x2 = x.reshape(m, d)
    outs = _front_call(x2, vec(norm_mix[0]), bf(w_in[0]), hg_lb.astype(F32), vec(hg_norm[0]),
                       batch=batch, seq=seq, tm=tm, dilations=dilations)
    nd = len(dilations)
    qs, ks, vs = outs[:nd], outs[nd:2 * nd], outs[2 * nd:3 * nd]
    yb, ga, gg = outs[3 * nd:]
    os_, lses = [], []
    for g, (window, dilation) in enumerate(A_PATTERNS):
        o, lse = _attn_call(qs[g], ks[g], vs[g], batch=batch, seq=seq, window=window, dilation=dilation)
        os_.append(o)
        lses.append(lse)
    out = _back_call(x2, os_, lses, yb, ga, gg, p[0].reshape(m, -1), bf(w_a_up[0]), bf(w_b_up[0]),
                     bf(w_out[0]), vec(norm_ffn[0]), vec(norm_ple[0]), vec(norm_final), bf(w_gu[0]),
                     bf(w_down[0]), bf(w_pe[0]), bf(w_pg[0]), tm=tm, dilations=dilations, chunk=256)
    return out.reshape(batch, seq, d)
```

```python
import functools

import jax
import jax.numpy as jnp
import numpy as np
from jax import lax
from jax.experimental import pallas as pl
from jax.experimental.pallas import tpu as pltpu

F32 = jnp.float32
BF16 = jnp.bfloat16

LANES = 128
EPS = 1e-6
NEG = -1e30
LOG2E = 1.4426950408889634

A_HEAD_DIM = 64
A_PATTERNS = ((128, 1), (512, 4), (2048, 16))
A_BLK = 128
ATTN_UNITS = 32
B_HEAD_DIM = 128
HG_TILE = 256

VMEM_LIMIT = 56 * 1024 * 1024


def _params(*sem):
    return pltpu.CompilerParams(dimension_semantics=sem, vmem_limit_bytes=VMEM_LIMIT)


def _const_spec(shape):
    nd = len(shape)
    return pl.BlockSpec(shape, lambda *_: (0,) * nd, pipeline_mode=pl.Buffered(1))


def _dot(a, b):
    return jnp.dot(a, b, preferred_element_type=F32)


def _dot_nt(a, b):
    return lax.dot_general(a, b, (((1,), (1,)), ((), ())), preferred_element_type=F32)


def _dot_tn(a, b):
    return lax.dot_general(a, b, (((0,), (0,)), ((), ())), preferred_element_type=F32)


def _rms(x, g):
    return x * lax.rsqrt(jnp.mean(x * x, axis=-1, keepdims=True) + EPS) * g


SUB = 8
HG_BLOCK = 128
HG_SAFE_LOG2 = 100.0


def _rep(t, n):
    return jnp.concatenate([t] * n, axis=0)


def _small_levels(lf2, visit=None):
    sub = lax.broadcasted_iota(jnp.int32, (1, SUB, lf2.shape[-1]), 1)
    pre = tot = lf2.reshape(lf2.shape[0] // SUB, SUB, lf2.shape[-1])
    m, level = 1, 1
    while m < SUB:
        upper = (sub & m) != 0
        if visit is not None:
            visit(level, upper, pre, tot)
        below = pltpu.roll(tot, m, 1)
        above = pltpu.roll(tot, SUB - m, 1)
        pre = pre + jnp.where(upper, below, 0.0)
        tot = tot + jnp.where(upper, below, above)
        m, level = 2 * m, level + 1
    return pre, tot


def _log_decay(lf2):
    sub = lax.broadcasted_iota(jnp.int32, (1, SUB, lf2.shape[-1]), 1)
    pre = lf2.reshape(lf2.shape[0] // SUB, SUB, lf2.shape[-1])
    k = 1
    while k < SUB:
        pre = pre + jnp.where(sub >= k, pltpu.roll(pre, k, 1), 0.0)
        k *= 2
    tot = jnp.broadcast_to(pre[:, SUB - 1:SUB, :], pre.shape)
    ends = [tot[0]]
    for g in range(1, pre.shape[0]):
        ends.append(ends[-1] + tot[g])
    b = jnp.concatenate([pre[0]] + [pre[g] + ends[g - 1] for g in range(1, len(ends))], axis=0)
    return b, ends


def _hgrn_unit(q, kk, lf2, b, ends, v, gact, gn, lvl, state_ref, head, block_start):
    tile = q.shape[0]
    groups, half = tile // SUB, tile // 2
    quads = [slice(r0, r0 + half) for r0 in (0, half)]

    def level_factors(m):
        parts = []
        for lo in range(0, tile, 2 * m):
            mid = _rep(ends[(lo + m) // SUB - 1], m // SUB)
            parts.append(kk[lo:lo + m] * jnp.exp2(mid - b[lo:lo + m]))
            parts.append(q[lo + m:lo + 2 * m] * jnp.exp2(b[lo + m:lo + 2 * m] - mid))
        return jnp.concatenate(parts, axis=0)

    def add_level(level, y):
        yb16 = y.astype(BF16)
        for i, rows in enumerate(quads):
            diag[i] = jnp.where(lvl == level, _dot_nt(yb16[rows], yb16[rows]), diag[i])

    if block_start:
        qs, ks = [], []
        for lo in range(0, tile, HG_BLOCK):
            d = b[lo:lo + HG_BLOCK]
            if lo:
                d = d - _rep(ends[lo // SUB - 1], HG_BLOCK // SUB)
            qs.append(q[lo:lo + HG_BLOCK] * jnp.exp2(d))
            ks.append(kk[lo:lo + HG_BLOCK] * jnp.exp2(-d))
        qf = jnp.concatenate(qs, axis=0).astype(BF16)
        kf = jnp.concatenate(ks, axis=0).astype(BF16)
        diag = [jnp.where(lvl >= 0, _dot_nt(qf[rows], kf[rows]), 0.0) for rows in quads]
        m, level = HG_BLOCK, HG_BLOCK.bit_length()
    else:
        qb16, kb16 = q.astype(BF16), kk.astype(BF16)
        diag = [jnp.where(lvl == 0, _dot_nt(qb16[rows], kb16[rows]), 0.0) for rows in quads]
        g3 = lambda t: t.reshape(groups, SUB, t.shape[-1])
        q3, k3 = g3(q), g3(kk)

        def small(level, upper, pre, tot):
            y3 = jnp.where(upper, q3, k3) * jnp.exp2(jnp.where(upper, pre, tot - pre))
            add_level(level, y3.reshape(tile, -1))

        _small_levels(lf2, small)
        m, level = SUB, SUB.bit_length()
    while m < half:
        add_level(level, level_factors(m))
        m, level = 2 * m, level + 1
    top = level_factors(half).astype(BF16)
    cross = _dot_nt(top[half:], top[:half]).astype(BF16)
    diag = [sc.astype(BF16) for sc in diag]

    st = state_ref[head]
    whole = _rep(ends[-1], groups)
    inter = _dot_nt((q * jnp.exp2(b)).astype(BF16), st.astype(BF16))
    o = inter + jnp.concatenate(
        [_dot(diag[0], v[:half]), _dot(jnp.concatenate([cross, diag[1]], axis=1), v)], axis=0)
    k_end = (kk * jnp.exp2(whole - b)).astype(BF16)
    state_ref[head] = st * jnp.exp2(ends[-1][0:1, :]) + _dot_tn(v, k_end)
    return _rms(o, gn) * gact


def _front_kernel(lvl_ref, hglb_ref, gn_ref, x_ref, g_ref, w_ref, *refs, width, dilations):
    nd = len(dilations)
    attn_refs = refs[:3 * nd]
    yb_ref, ga_ref, gg_ref, hx_ref, slab_ref, state_ref = refs[3 * nd:]
    tm = x_ref.shape[0]
    heads = width // B_HEAD_DIM

    @pl.when(pl.program_id(1) == 0)
    def _():
        state_ref[...] = jnp.zeros_like(state_ref)

    hx_ref[...] = _rms(x_ref[...], g_ref[...]).astype(BF16)
    half = tm // 2
    slabs = width // LANES

    def proj(t0, rows, c0, n):
        return _dot(hx_ref[pl.ds(t0, rows), :], w_ref[:, c0:c0 + n])

    def attn_job(a, t0):
        y = proj(t0, half, a * width, width)
        if a == 0:
            y = y * (A_HEAD_DIM ** -0.5 * LOG2E)
        for c in range(slabs):
            slab_ref[0, a * slabs + c, pl.ds(t0, half), :] = y[:, c * LANES:(c + 1) * LANES]
        d_prev = 1
        for k, (out_ref, d) in enumerate(zip(attn_refs[a * nd:(a + 1) * nd], dilations)):
            if d == 1:
                out_ref[pl.ds(t0, half), :] = y.astype(BF16)
                continue
            ratio, n_prev, n = d // d_prev, half // d_prev, half // d
            for r in range(d):
                r_prev, j = r % d_prev, r // d_prev
                for c in range(slabs):
                    piece = slab_ref[k - 1, a * slabs + c,
                                     pl.ds(t0 + r_prev * n_prev + j, n, stride=ratio), :]
                    out_ref[pl.ds(t0 // d, n), pl.ds((r * slabs + c) * LANES, LANES)] = piece.astype(BF16)
                    if k + 1 < nd:
                        slab_ref[k, a * slabs + c, pl.ds(t0 + r * n, n), :] = piece
            d_prev = d

    def gate_job(out_ref, c0, j, t0):
        y = proj(t0, half, c0 + j * width, width)
        out_ref[pl.ds(t0, half), pl.ds(j * width, width)] = jax.nn.sigmoid(y).astype(BF16)

    jobs = [functools.partial(attn_job, a, t0) for a in range(3) for t0 in (0, half)]
    jobs += [functools.partial(gate_job, ref, c0, j, t0)
             for ref, c0 in ((ga_ref, 7 * width), (gg_ref, 9 * width)) for j in range(2)
             for t0 in (0, half)]

    hg = hglb_ref[...]
    ex = jnp.exp(hg - jnp.max(hg, axis=0, keepdims=True))
    lb_all = ex[0:1] / jnp.sum(ex, axis=0, keepdims=True)
    lvl = lvl_ref[...]

    def head_proj(hd):
        w_head = jnp.concatenate(
            [w_ref[:, pl.ds((3 + j) * width + hd * B_HEAD_DIM, B_HEAD_DIM)] for j in range(4)], axis=1)
        hp = _dot(hx_ref[...], w_head)
        q, fgate, v, gate = (hp[:, j * B_HEAD_DIM:(j + 1) * B_HEAD_DIM] for j in range(4))
        lb = lb_all[:, hd * B_HEAD_DIM:(hd + 1) * B_HEAD_DIM]
        f = lb + (1.0 - lb) * jax.nn.sigmoid(fgate)
        return q, 1.0 - f, jnp.log2(f), v.astype(BF16), gate * jax.nn.sigmoid(gate)

    units, worst = [], None
    for hd in range(heads):
        data = head_proj(hd)
        for t0 in range(0, tm, HG_TILE):
            q, kk, lf2, v, gact = (t[t0:t0 + HG_TILE] for t in data)
            b, ends = _log_decay(lf2)
            for lo in range(0, HG_TILE, HG_BLOCK):
                blk = ends[(lo + HG_BLOCK) // SUB - 1]
                if lo:
                    blk = blk - ends[lo // SUB - 1]
                worst = blk if worst is None else jnp.minimum(worst, blk)
            units.append((hd, t0, q, kk, lf2, b, ends, v, gact))
    for job in jobs:
        job()

    def run(block_start):
        for hd, t0, *args in units:
            y = _hgrn_unit(*args, gn_ref[...], lvl, state_ref, hd, block_start)
            yb_ref[pl.ds(t0, HG_TILE), pl.ds(hd * B_HEAD_DIM, B_HEAD_DIM)] = y.astype(yb_ref.dtype)

    safe = jnp.min(worst) >= -HG_SAFE_LOG2
    pl.when(safe)(functools.partial(run, True))
    pl.when(jnp.logical_not(safe))(functools.partial(run, False))


def _level_map(tile):
    t = np.arange(tile)[:, None]
    s = np.arange(tile)[None, :]
    x = np.maximum(t ^ s, 1)
    lv = np.floor(np.log2(x)).astype(np.int32) + 1
    return jnp.asarray(np.where(s < t, lv, np.where(s == t, 0, -1)), jnp.int32)


def _front_call(x2, g, w, hg_lb, g_norm, *, batch, seq, tm, dilations):
    m, d = x2.shape
    width = d // 2
    assert seq % tm == 0 and tm % HG_TILE == 0
    assert dilations[0] == 1 and all(b % a == 0 for a, b in zip(dilations, dilations[1:]))
    steps = seq // tm
    row = lambda n: pl.BlockSpec((tm, n), lambda b, t: (b * steps + t, 0))
    sds = lambda n, dt: jax.ShapeDtypeStruct((m, n), dt)
    dil_specs = [pl.BlockSpec((tm // dl, dl * width), lambda b, t: (b * steps + t, 0))
                 for dl in dilations] * 3
    dil_shapes = [jax.ShapeDtypeStruct((m // dl, dl * width), BF16) for dl in dilations] * 3
    return pl.pallas_call(
        functools.partial(_front_kernel, width=width, dilations=dilations),
        grid=(batch, steps),
        in_specs=[_const_spec((HG_TILE // 2, HG_TILE // 2)), _const_spec(hg_lb.shape),
                  _const_spec((1, B_HEAD_DIM)), row(d), _const_spec((1, d)), _const_spec(w.shape)],
        out_specs=dil_specs + [row(width), row(d), row(d)],
        out_shape=dil_shapes + [sds(width, BF16), sds(d, BF16), sds(d, BF16)],
        scratch_shapes=[pltpu.VMEM((tm, d), BF16),
                        pltpu.VMEM((len(dilations) - 1, 3 * width // LANES, tm, LANES), F32),
                        pltpu.VMEM((width // B_HEAD_DIM, B_HEAD_DIM, B_HEAD_DIM), F32)],
        compiler_params=_params("parallel", "arbitrary"),
        name="front",
    )(_level_map(HG_TILE // 2), hg_lb, g_norm, x2, g, w)


def _attn_kernel(bias_ref, q_ref, k_ref, v_ref, o_ref, lse_ref, *, tq, rblk):
    width = q_ref.shape[2] // rblk
    pairs = width // LANES
    lane = lax.broadcasted_iota(jnp.int32, (A_BLK, LANES), 1)
    low_half = lane < A_HEAD_DIM
    ones = jnp.ones((2 * A_BLK, LANES), BF16)
    for rr in range(rblk):
        for qi in range(tq // A_BLK):
            n = pl.program_id(2) * (tq // A_BLK) + qi
            start = pl.multiple_of(jnp.maximum(n - 1, 0) * A_BLK, A_BLK)
            bias = bias_ref[jnp.where(n == 0, 1, 0)]
            rows = pl.ds(qi * A_BLK, A_BLK)
            mx_tile = jnp.zeros((A_BLK, LANES), F32)
            den_tile = jnp.ones((A_BLK, LANES), F32)
            for p in range(pairs):
                cols = pl.ds(rr * width + p * LANES, LANES)
                q_pair = q_ref[0, rows, cols]
                k_band = k_ref[0, pl.ds(start, 2 * A_BLK), cols]
                v_cat = jnp.concatenate([v_ref[0, pl.ds(start, 2 * A_BLK), cols], ones], axis=1)
                nums, dens = [], []
                for hh in range(2):
                    mine = low_half if hh == 0 else jnp.logical_not(low_half)
                    s = _dot_nt(jnp.where(mine, q_pair, jnp.zeros_like(q_pair)), k_band) + bias
                    mx = jnp.max(s, axis=-1, keepdims=True)
                    r = _dot(jnp.exp2(s - mx).astype(BF16), v_cat)
                    nums.append(r[:, :LANES])
                    dens.append(r[:, LANES:])
                    mx_tile = jnp.where(lane == 2 * p + hh, mx, mx_tile)
                    den_tile = jnp.where(lane == 2 * p + hh, dens[-1], den_tile)
                o_ref[0, rows, cols] = (jnp.where(low_half, nums[0], nums[1])
                                        / jnp.where(low_half, dens[0], dens[1])).astype(o_ref.dtype)
            lse_ref[0, rows, pl.ds(rr * LANES, LANES)] = mx_tile + jnp.log2(den_tile)


def _band_bias(span):
    a = np.arange(A_BLK)[:, None]
    c = np.arange(2 * A_BLK)[None, :]
    rel = a + A_BLK - c
    normal = (rel >= 0) & (rel <= span)
    rel0 = a - c
    first = (rel0 >= 0) & (rel0 <= span)
    return jnp.asarray(np.where(np.stack([normal, first]), 0.0, NEG), F32)


def _attn_call(q, k, v, *, batch, seq, window, dilation):
    width = q.shape[1] // dilation
    assert seq % (dilation * A_BLK) == 0 and window % dilation == 0
    length = seq // dilation
    tq = min(length, ATTN_UNITS * A_BLK)
    rblk = min(dilation, ATTN_UNITS * A_BLK // tq)
    sub = lambda t: t.reshape(batch, length, dilation * width)
    o, lse = pl.pallas_call(
        functools.partial(_attn_kernel, tq=tq, rblk=rblk),
        grid=(batch, dilation // rblk, length // tq),
        in_specs=[
            _const_spec((2, A_BLK, 2 * A_BLK)),
            pl.BlockSpec((1, tq, rblk * width), lambda b, r, t: (b, t, r)),
            pl.BlockSpec((1, length, rblk * width), lambda b, r, t: (b, 0, r)),
            pl.BlockSpec((1, length, rblk * width), lambda b, r, t: (b, 0, r)),
        ],
        out_specs=[
            pl.BlockSpec((1, tq, rblk * width), lambda b, r, t: (b, t, r)),
            pl.BlockSpec((1, tq, rblk * LANES), lambda b, r, t: (b, t, r)),
        ],
        out_shape=[
            jax.ShapeDtypeStruct((batch, length, dilation * width), BF16),
            jax.ShapeDtypeStruct((batch, length, dilation * LANES), F32),
        ],
        compiler_params=_params("parallel", "parallel", "arbitrary"),
        name=f"attn_d{dilation}",
    )(_band_bias(window // dilation), sub(q), sub(k), sub(v))
    return (o.reshape(batch * length, dilation * width), lse.reshape(batch * length, dilation * LANES))


def _merge_body(x_ref, o1_ref, o2_ref, o3_ref, l1_ref, l2_ref, l3_ref, yb_ref, ga_ref, gg_ref,
                ex_ref, wa_ref, wb_ref, wo_ref, oslab_ref, lslab_ref, dilations):
    tm = x_ref.shape[0]
    slabs = oslab_ref.shape[1]
    b = _dot(yb_ref[...], wb_ref[...])

    def natural(piece, k, nslab, bufs):
        src = piece
        for level in range(k, 0, -1):
            d, d_prev = dilations[level], dilations[level - 1]
            ratio, n, n_prev = d // d_prev, tm // d, tm // d_prev
            dst = bufs.at[(level - 1) % 2]
            for r in range(d):
                r_prev, j = r % d_prev, r // d_prev
                for c in range(nslab):
                    dst[c, pl.ds(r_prev * n_prev + j, n, stride=ratio), :] = src(r, c)
            src = lambda r, c, dst=dst, n_prev=n_prev: dst[c, pl.ds(r * n_prev, n_prev), :]
        return [bufs[0, c] for c in range(nslab)]

    lses = []
    for k, l_ref in enumerate((l1_ref, l2_ref, l3_ref)):
        if k == 0:
            lses.append(l_ref[...])
        else:
            lses.append(natural(lambda r, c, l_ref=l_ref: l_ref[:, pl.ds(r * LANES, LANES)],
                                k, 1, lslab_ref)[0])
    top = jnp.maximum(jnp.maximum(lses[0], lses[1]), lses[2])
    ws = [jnp.exp2(l - top) for l in lses]
    inv = 1.0 / (ws[0] + ws[1] + ws[2])
    ya = None
    for k, (w, o_ref) in enumerate(zip(ws, (o1_ref, o2_ref, o3_ref))):
        if k == 0:
            o = o_ref[...].astype(F32)
        else:
            o = jnp.concatenate(natural(
                lambda r, c, o_ref=o_ref: o_ref[:, pl.ds((r * slabs + c) * LANES, LANES)].astype(F32),
                k, slabs, oslab_ref), axis=1)
        term = _dot((w * inv).astype(BF16), ex_ref[...]) * o
        ya = term if ya is None else ya + term
    a = _dot(ya.astype(BF16), wa_ref[...])
    merged = ga_ref[...].astype(F32) * a + gg_ref[...].astype(F32) * b
    return x_ref[...] + _dot(merged.astype(BF16), wo_ref[...])


def _ffn_body(x, p_ref, nf_ref, np_ref, nl_ref, wgu_ref, wd_ref, wpe_ref, wpg_ref, chunk):
    h = _rms(x, nf_ref[...]).astype(BF16)
    hidden = wd_ref.shape[0]
    acc = x
    for c in range(0, hidden, chunk):
        g = _dot(h, wgu_ref[:, c:c + chunk])
        u = _dot(h, wgu_ref[:, hidden + c:hidden + c + chunk])
        acc = acc + _dot((g * jax.nn.sigmoid(g) * u).astype(BF16), wd_ref[c:c + chunk, :])
    hp = _rms(acc, np_ref[...]).astype(BF16)
    pe = _dot(p_ref[...].astype(BF16), wpe_ref[...])
    x3 = acc + pe * jax.nn.sigmoid(_dot(hp, wpg_ref[...]))
    return _rms(x3, nl_ref[...])


def _back_kernel(*refs, dilations, chunk):
    rows, p_ref, consts, norms, ffn_w = refs[:10], refs[10], refs[11:15], refs[15:18], refs[18:22]
    out_ref, oslab_ref, lslab_ref = refs[22:]
    x1 = _merge_body(*rows, *consts, oslab_ref, lslab_ref, dilations)
    out_ref[...] = _ffn_body(x1, p_ref, *norms, *ffn_w, chunk)


def _back_call(x2, os_, lses, yb, ga, gg, p2, wa, wb, wo, nf, npl, nl, wgu, wd, wpe, wpg, *, tm,
               dilations, chunk):
    m, d = x2.shape
    width = yb.shape[1]
    assert wd.shape[0] % chunk == 0 and wgu.shape[1] == 2 * wd.shape[0]
    expand = np.zeros((LANES, width), np.float32)
    for h in range(width // A_HEAD_DIM):
        expand[h, h * A_HEAD_DIM:(h + 1) * A_HEAD_DIM] = 1.0
    expand = jnp.asarray(expand, BF16)
    row = lambda n: pl.BlockSpec((tm, n), lambda i: (i, 0))
    dil = lambda n: [pl.BlockSpec((tm // dl, dl * n), lambda i: (i, 0)) for dl in dilations]
    consts = (expand, wa, wb, wo, nf, npl, nl, wgu, wd, wpe, wpg)
    return pl.pallas_call(
        functools.partial(_back_kernel, dilations=dilations, chunk=chunk),
        grid=(m // tm,),
        in_specs=[row(d)] + dil(width) + dil(LANES) + [row(width), row(d), row(d), row(p2.shape[1])]
        + [_const_spec(c.shape) for c in consts],
        out_specs=row(d),
        out_shape=jax.ShapeDtypeStruct((m, d), F32),
        scratch_shapes=[pltpu.VMEM((2, width // LANES, tm, LANES), F32),
                        pltpu.VMEM((2, 1, tm, LANES), F32)],
        compiler_params=_params("parallel"),
        name="back",
    )(x2, *os_, *lses, yb, ga, gg, p2, *consts)


def kernel(x, p, norm_mix, w_in, hg_lb, hg_norm, w_a_up, w_b_up, w_out, norm_ffn, w_gu, w_down,
           norm_ple, w_pe, w_pg, norm_final):
    batch, seq, d = x.shape
    depth = p.shape[0]
    assert depth == 1 and hg_lb.shape[0] == 2
    m = batch * seq
    tm = 512
    bf = lambda w: w.astype(BF16)
    vec = lambda g: g.reshape(1, -1).astype(F32)

    dilations = tuple(dl for _, dl in A_PATTERNS)
    x2 = x.reshape(m, d)
    outs = _front_call(x2, vec(norm_mix[0]), bf(w_in[0]), hg_lb.astype(F32), vec(hg_norm[0]),
                       batch=batch, seq=seq, tm=tm, dilations=dilations)
    nd = len(dilations)
    qs, ks, vs = outs[:nd], outs[nd:2 * nd], outs[2 * nd:3 * nd]
    yb, ga, gg = outs[3 * nd:]
    os_, lses = [], []
    for g, (window, dilation) in enumerate(A_PATTERNS):
        o, lse = _attn_call(qs[g], ks[g], vs[g], batch=batch, seq=seq, window=window, dilation=dilation)
        os_.append(o)
        lses.append(lse)
    out = _back_call(x2, os_, lses, yb, ga, gg, p[0].reshape(m, -1), bf(w_a_up[0]), bf(w_b_up[0]),
                     bf(w_out[0]), vec(norm_ffn[0]), vec(norm_ple[0]), vec(norm_final), bf(w_gu[0]),
                     bf(w_down[0]), bf(w_pe[0]), bf(w_pg[0]), tm=tm, dilations=dilations, chunk=256)
    return out.reshape(batch, seq, d)
```
